```python
import math
import jax, jax.numpy as jnp
from jax import lax
import numpy as np

D_MODEL = 2048
BATCH = 4
SEQ = 2048
DEPTH = 2
DEC_BATCH = 128
DEC_SEQ = 4
PAST_LEN = 16384
PAGE_SIZE = 128

N_META = 16
D_FF = 5632
CONV_KERNEL = 31
CONV_BUF = CONV_KERNEL - 1
D_CONV = D_MODEL
GLA_HEADS = 4
GLA_DK = D_MODEL // 2 // GLA_HEADS
GLA_DV = D_MODEL // GLA_HEADS
GLA_GATE_RANK = 16
GLA_TAU = 16.0
GLA_CHUNK = 64
N_CONV_LAYERS = (DEPTH + 1) // 2
N_GLA_LAYERS = DEPTH // 2
EPS = 1e-6

kernel_name = "conformer_conv_gla_macaron_hybrid_step"


def rms_norm(x, g):
    xf = x.astype(jnp.float32)
    y = xf * lax.rsqrt(jnp.mean(xf * xf, axis=-1, keepdims=True) + EPS)
    return (y * g.astype(jnp.float32)).astype(x.dtype)


def swiglu_ffn(h, w_gate, w_up, w_down):
    return (jax.nn.silu(h @ w_gate) * (h @ w_up)) @ w_down


def conv_module(h, buf, w_pw1, b_pw1, w_dw, b_dw, ln_g, ln_b, w_pw2, b_pw2):
    a = h @ w_pw1 + b_pw1
    u = a[..., :D_CONV] * jax.nn.sigmoid(a[..., D_CONV:])
    ext = jnp.concatenate([buf.astype(u.dtype), u], axis=1)
    y = lax.conv_general_dilated(
        ext, w_dw.astype(u.dtype)[:, None, :], window_strides=(1,), padding='VALID',
        dimension_numbers=('NWC', 'WIO', 'NWC'), feature_group_count=D_CONV) + b_dw
    yf = y.astype(jnp.float32)
    mu = jnp.mean(yf, axis=-1, keepdims=True)
    var = jnp.mean(jnp.square(yf - mu), axis=-1, keepdims=True)
    yn = ((yf - mu) * lax.rsqrt(var + EPS) * ln_g.astype(jnp.float32) + ln_b.astype(jnp.float32)).astype(h.dtype)
    out = jax.nn.silu(yn) @ w_pw2 + b_pw2
    return out, ext[:, -CONV_BUF:].astype(buf.dtype)


def gla_chunk(S0, q, k, v, la):
    L = q.shape[2]
    b = jnp.cumsum(la, axis=2)
    inter = jnp.einsum('bhtd,bhdv->bhtv', q * jnp.exp(b), S0)
    mask = jnp.tril(jnp.ones((L, L), dtype=bool))
    diff = b[:, :, :, None, :] - b[:, :, None, :, :]
    decay = jnp.exp(jnp.where(mask[:, :, None], diff, -jnp.inf))
    scores = jnp.einsum('bhtsd,bhsd->bhts', q[:, :, :, None, :] * decay, k)
    intra = jnp.einsum('bhts,bhsv->bhtv', scores, v)
    b_last = b[:, :, -1:, :]
    S = jnp.exp(b_last[:, :, 0, :])[..., None] * S0 + jnp.einsum(
        'bhsd,bhsv->bhdv', k * jnp.exp(b_last - b), v)
    return S, inter + intra


def gla_module(h, S0, lead, w_q, w_k, w_v, w_g1, w_g2, b_g, w_r, gn_g, w_o):
    B, L, _ = h.shape
    f32 = jnp.float32

    def heads(t, d):
        return t.reshape(B, L, GLA_HEADS, d).transpose(0, 2, 1, 3).astype(f32)

    q = heads(h @ w_q, GLA_DK) * (GLA_DK ** -0.5)
    k = heads(h @ w_k, GLA_DK)
    v = heads(h @ w_v, GLA_DV)
    la = jax.nn.log_sigmoid(heads((h @ w_g1) @ w_g2 + b_g, GLA_DK)) / GLA_TAU
    S = S0.astype(f32)
    parts = []
    if lead > 0:
        S, o_lead = gla_chunk(S, q[:, :, :lead], k[:, :, :lead], v[:, :, :lead], la[:, :, :lead])
        parts.append(o_lead)
    rest = L - lead
    c = math.gcd(GLA_CHUNK, rest)
    n = rest // c

    def to_blocks(t):
        return t[:, :, lead:].reshape(B, GLA_HEADS, n, c, t.shape[-1]).transpose(2, 0, 1, 3, 4)

    def step(S_c, xs):
        return gla_chunk(S_c, *xs)

    S, o_rest = lax.scan(step, S, (to_blocks(q), to_blocks(k), to_blocks(v), to_blocks(la)))
    parts.append(o_rest.transpose(1, 2, 0, 3, 4).reshape(B, GLA_HEADS, rest, GLA_DV))
    o = jnp.concatenate(parts, axis=2).transpose(0, 2, 1, 3)
    o = o * lax.rsqrt(jnp.mean(o * o, axis=-1, keepdims=True) + EPS) * gn_g.astype(f32)
    r = jax.nn.silu((h @ w_r).astype(f32)).reshape(B, L, GLA_HEADS, GLA_DV)
    out = (o * r).reshape(B, L, GLA_HEADS * GLA_DV).astype(h.dtype) @ w_o
    return out, S.astype(S0.dtype)


def trunk(x, conv_bufs, gla_states, lead, p):
    new_conv, new_gla = [], []
    for i in range(DEPTH):
        j = i // 2
        x = x + 0.5 * swiglu_ffn(rms_norm(x, p['norm_ffn'][i, 0]), p['w_ffn_gate'][i, 0],
                                 p['w_ffn_up'][i, 0], p['w_ffn_down'][i, 0])
        h = rms_norm(x, p['norm_mix'][i])
        if i % 2 == 0:
            out, st = conv_module(h, conv_bufs[j], p['conv_w_pw1'][j], p['conv_b_pw1'][j],
                                  p['conv_w_dw'][j], p['conv_b_dw'][j], p['conv_ln_g'][j],
                                  p['conv_ln_b'][j], p['conv_w_pw2'][j], p['conv_b_pw2'][j])
            new_conv.append(st)
        else:
            out, st = gla_module(h, gla_states[j], lead, p['gla_w_q'][j], p['gla_w_k'][j],
                                 p['gla_w_v'][j], p['gla_w_g1'][j], p['gla_w_g2'][j],
                                 p['gla_b_g'][j], p['gla_w_r'][j], p['gla_gn_g'][j], p['gla_w_o'][j])
            new_gla.append(st)
        x = x + out
        x = x + 0.5 * swiglu_ffn(rms_norm(x, p['norm_ffn'][i, 1]), p['w_ffn_gate'][i, 1],
                                 p['w_ffn_up'][i, 1], p['w_ffn_down'][i, 1])
    return rms_norm(x, p['norm_final']), jnp.stack(new_conv), jnp.stack(new_gla)


def setup_inputs(seed: int = 0) -> dict:
    key = jax.random.key(seed)
    ks = iter(jax.random.split(key, 32))
    f32 = jnp.float32

    def nrm(shape, scale):
        return jax.random.normal(next(ks), shape, f32) * scale

    Nc, Ng, D = N_CONV_LAYERS, N_GLA_LAYERS, D_MODEL
    return {
        'x_prompt': nrm((BATCH, SEQ, D), 1.0),
        'x_sample': nrm((DEC_BATCH, DEC_SEQ, D), 1.0),
        'state_conv': nrm((Nc, DEC_BATCH, CONV_BUF, D_CONV), 0.5),
        'state_gla': nrm((Ng, DEC_BATCH, GLA_HEADS, GLA_DK, GLA_DV), 1.0),
        'meta_tokens': nrm((N_META, D), 1.0),
        'norm_ffn': 1.0 + nrm((DEPTH, 2, D), 0.02),
        'w_ffn_gate': nrm((DEPTH, 2, D, D_FF), D ** -0.5),
        'w_ffn_up': nrm((DEPTH, 2, D, D_FF), D ** -0.5),
        'w_ffn_down': nrm((DEPTH, 2, D_FF, D), D_FF ** -0.5),
        'norm_mix': 1.0 + nrm((DEPTH, D), 0.02),
        'conv_w_pw1': nrm((Nc, D, 2 * D_CONV), D ** -0.5),
        'conv_b_pw1': nrm((Nc, 2 * D_CONV), 0.02),
        'conv_w_dw': nrm((Nc, CONV_KERNEL, D_CONV), CONV_KERNEL ** -0.5),
        'conv_b_dw': nrm((Nc, D_CONV), 0.02),
        'conv_ln_g': 1.0 + nrm((Nc, D_CONV), 0.02),
        'conv_ln_b': nrm((Nc, D_CONV), 0.02),
        'conv_w_pw2': nrm((Nc, D_CONV, D), D_CONV ** -0.5),
        'conv_b_pw2': nrm((Nc, D), 0.02),
        'gla_w_q': nrm((Ng, D, GLA_HEADS * GLA_DK), D ** -0.5),
        'gla_w_k': nrm((Ng, D, GLA_HEADS * GLA_DK), D ** -0.5),
        'gla_w_v': nrm((Ng, D, GLA_HEADS * GLA_DV), D ** -0.5),
        'gla_w_g1': nrm((Ng, D, GLA_GATE_RANK), D ** -0.5),
        'gla_w_g2': nrm((Ng, GLA_GATE_RANK, GLA_HEADS * GLA_DK), GLA_GATE_RANK ** -0.5),
        'gla_b_g': nrm((Ng, GLA_HEADS * GLA_DK), 0.02),
        'gla_w_r': nrm((Ng, D, GLA_HEADS * GLA_DV), D ** -0.5),
        'gla_gn_g': 1.0 + nrm((Ng, GLA_DV), 0.02),
        'gla_w_o': nrm((Ng, GLA_HEADS * GLA_DV, D), (GLA_HEADS * GLA_DV) ** -0.5),
        'norm_final': 1.0 + nrm((D,), 0.02),
    }


def reference(x_prompt, x_sample, state_conv, state_gla, meta_tokens, norm_ffn, w_ffn_gate,
              w_ffn_up, w_ffn_down, norm_mix, conv_w_pw1, conv_b_pw1, conv_w_dw, conv_b_dw,
              conv_ln_g, conv_ln_b, conv_w_pw2, conv_b_pw2, gla_w_q, gla_w_k, gla_w_v, gla_w_g1,
              gla_w_g2, gla_b_g, gla_w_r, gla_gn_g, gla_w_o, norm_final):
    p = dict(norm_ffn=norm_ffn, w_ffn_gate=w_ffn_gate, w_ffn_up=w_ffn_up, w_ffn_down=w_ffn_down,
             norm_mix=norm_mix, conv_w_pw1=conv_w_pw1, conv_b_pw1=conv_b_pw1, conv_w_dw=conv_w_dw,
             conv_b_dw=conv_b_dw, conv_ln_g=conv_ln_g, conv_ln_b=conv_ln_b, conv_w_pw2=conv_w_pw2,
             conv_b_pw2=conv_b_pw2, gla_w_q=gla_w_q, gla_w_k=gla_w_k, gla_w_v=gla_w_v,
             gla_w_g1=gla_w_g1, gla_w_g2=gla_w_g2, gla_b_g=gla_b_g, gla_w_r=gla_w_r,
             gla_gn_g=gla_gn_g, gla_w_o=gla_w_o, norm_final=norm_final)
    B = x_prompt.shape[0]
    meta = jnp.broadcast_to(meta_tokens.astype(x_prompt.dtype)[None], (B, N_META, D_MODEL))
    xp = jnp.concatenate([meta, x_prompt], axis=1)
    conv0 = [jnp.zeros((B, CONV_BUF, D_CONV), state_conv.dtype) for _ in range(N_CONV_LAYERS)]
    gla0 = [jnp.zeros((B, GLA_HEADS, GLA_DK, GLA_DV), state_gla.dtype) for _ in range(N_GLA_LAYERS)]
    yp, new_conv_prompt, new_gla_prompt = trunk(xp, conv0, gla0, N_META, p)
    y_prompt = yp[:, N_META:]
    conv_s = [state_conv[j] for j in range(N_CONV_LAYERS)]
    gla_s = [state_gla[j] for j in range(N_GLA_LAYERS)]
    y_sample, new_conv_sample, new_gla_sample = trunk(x_sample, conv_s, gla_s, 0, p)
    return (y_prompt, y_sample, new_conv_prompt, new_gla_prompt, new_conv_sample, new_gla_sample)
```

```python
import functools

import jax
import jax.numpy as jnp
from jax import lax
from jax.experimental import pallas as pl
from jax.experimental.pallas import tpu as pltpu

F32 = jnp.float32
EPS = 1e-6
N_META_ROWS = 16
CONV_TAPS = 31
CONV_HALO = CONV_TAPS - 1
HALO_PAD = 32
GLA_HEADS = 4
GLA_SUB = 16
GLA_CHUNK = 64
GLA_TAU = 16.0
V7X_VMEM_LIMIT = 60000 * 1024

TOKEN_TILE = 896
FFN_TILE = 256
PROJ_TILE = 512
CONV_LANE_TILE = 256


def _params(*sem):
    return pltpu.CompilerParams(dimension_semantics=sem, vmem_limit_bytes=V7X_VMEM_LIMIT)


def _rms(x, g):
    return x * lax.rsqrt(jnp.mean(x * x, axis=-1, keepdims=True) + EPS) * g


def _silu(x):
    return x * jax.nn.sigmoid(x)


def _dot(a, b):
    return jnp.dot(a, b, preferred_element_type=F32)


def _ffn_kernel(x_ref, g_ref, wg_ref, wu_ref, wd_ref, gf_ref, o_ref, h_ref, *, final_norm):
    f = pl.program_id(1)

    @pl.when(f == 0)
    def _():
        x = x_ref[...]
        h_ref[...] = _rms(x, g_ref[...])
        o_ref[...] = x

    h = h_ref[...]
    a = 0.5 * _silu(_dot(h, wg_ref[...])) * _dot(h, wu_ref[...])
    o_ref[...] += _dot(a, wd_ref[...])

    if final_norm:
        @pl.when(f == pl.num_programs(1) - 1)
        def _():
            o_ref[...] = _rms(o_ref[...], gf_ref[...])


def _ffn(x, g, wg, wu, wd, gf, *, final_norm):
    t, d = x.shape
    f = wg.shape[1]
    tm, tf = TOKEN_TILE, FFN_TILE
    return pl.pallas_call(
        functools.partial(_ffn_kernel, final_norm=final_norm),
        grid=(t // tm, f // tf),
        in_specs=[
            pl.BlockSpec((tm, d), lambda i, j: (i, 0)),
            pl.BlockSpec((1, d), lambda i, j: (0, 0)),
            pl.BlockSpec((d, tf), lambda i, j: (0, j)),
            pl.BlockSpec((d, tf), lambda i, j: (0, j)),
            pl.BlockSpec((tf, d), lambda i, j: (j, 0)),
            pl.BlockSpec((1, d), lambda i, j: (0, 0)),
        ],
        out_specs=pl.BlockSpec((tm, d), lambda i, j: (i, 0)),
        out_shape=jax.ShapeDtypeStruct((t, d), F32),
        scratch_shapes=[pltpu.VMEM((tm, d), F32)],
        compiler_params=_params("parallel", "arbitrary"),
        name="ffn_final" if final_norm else "ffn",
    )(x, g.reshape(1, d), wg, wu, wd, gf.reshape(1, d))


def _pw1_kernel(x_ref, g_ref, wa_ref, wb_ref, ba_ref, bb_ref, u_ref, h_ref):
    @pl.when(pl.program_id(1) == 0)
    def _():
        h_ref[...] = _rms(x_ref[...], g_ref[...])

    h = h_ref[...]
    a = _dot(h, wa_ref[...]) + ba_ref[...]
    b = _dot(h, wb_ref[...]) + bb_ref[...]
    u_ref[...] = a * jax.nn.sigmoid(b)


def _pw1_glu(x, g, w, b):
    t, d = x.shape
    dc = w.shape[1] // 2
    tm, tn = TOKEN_TILE, PROJ_TILE
    nc = dc // tn
    b2 = b.reshape(1, 2 * dc)
    return pl.pallas_call(
        _pw1_kernel,
        grid=(t // tm, nc),
        in_specs=[
            pl.BlockSpec((tm, d), lambda i, j: (i, 0)),
            pl.BlockSpec((1, d), lambda i, j: (0, 0)),
            pl.BlockSpec((d, tn), lambda i, j: (0, j)),
            pl.BlockSpec((d, tn), lambda i, j: (0, j + nc)),
            pl.BlockSpec((1, tn), lambda i, j: (0, j)),
            pl.BlockSpec((1, tn), lambda i, j: (0, j + nc)),
        ],
        out_specs=pl.BlockSpec((tm, tn), lambda i, j: (i, j)),
        out_shape=jax.ShapeDtypeStruct((t, dc), F32),
        scratch_shapes=[pltpu.VMEM((tm, d), F32)],
        compiler_params=_params("parallel", "arbitrary"),
        name="conv_pw1_glu",
    )(x, g.reshape(1, d), w, w, b2, b2)


def _conv_window(win, w_ref, b_ref, rc):
    lead = HALO_PAD - CONV_HALO
    acc = jnp.broadcast_to(b_ref[...], (rc, win.shape[1]))
    for s in range(8):
        part = None
        for a in range((CONV_TAPS + lead + 7) // 8):
            k = 8 * a + s - lead
            if 0 <= k < CONV_TAPS:
                term = w_ref[k:k + 1, :] * win[8 * a:8 * a + rc + 8]
                part = term if part is None else part + term
        acc = acc + part[s:s + rc]
    return acc


def _dwconv_prompt_kernel(cur_ref, w_ref, b_ref, y_ref, ext_ref, *, rc):
    lp, tl = cur_ref.shape
    ext_ref[0:HALO_PAD, :] = jnp.zeros((HALO_PAD, tl), F32)
    ext_ref[HALO_PAD:HALO_PAD + lp, :] = cur_ref[...]
    ext_ref[HALO_PAD + lp:HALO_PAD + lp + 8, :] = jnp.zeros((8, tl), F32)

    def body(c, carry):
        r0 = pl.multiple_of(c * rc, 8)
        y_ref[pl.ds(r0, rc), :] = _conv_window(ext_ref[pl.ds(r0, rc + HALO_PAD + 8), :], w_ref, b_ref, rc)
        return carry

    lax.fori_loop(0, lp // rc, body, 0)


def _dwconv_prompt(u, w, b, *, n_seq, seq_len):
    d = u.shape[1]
    tl = CONV_LANE_TILE
    rc = max(r for r in range(8, 65, 8) if seq_len % r == 0)
    return pl.pallas_call(
        functools.partial(_dwconv_prompt_kernel, rc=rc),
        grid=(n_seq, d // tl),
        in_specs=[
            pl.BlockSpec((seq_len, tl), lambda i, j: (i, j)),
            pl.BlockSpec((CONV_TAPS, tl), lambda i, j: (0, j)),
            pl.BlockSpec((1, tl), lambda i, j: (0, j)),
        ],
        out_specs=pl.BlockSpec((seq_len, tl), lambda i, j: (i, j)),
        out_shape=jax.ShapeDtypeStruct((n_seq * seq_len, d), F32),
        scratch_shapes=[pltpu.VMEM((HALO_PAD + seq_len + 8, tl), F32)],
        compiler_params=_params("parallel", "parallel"),
        name="dwconv_prompt",
    )(u, w, b.reshape(1, d))


def _dwconv_sample_kernel(st_ref, u_ref, w_ref, b_ref, y_ref):
    ls, bs, tl = u_ref.shape
    for t in range(ls):
        acc = jnp.broadcast_to(b_ref[...], (bs, tl))
        for k in range(CONV_TAPS):
            j = t + k
            row = st_ref[j] if j < CONV_HALO else u_ref[j - CONV_HALO]
            acc = acc + w_ref[k:k + 1, :] * row
        y_ref[t] = acc


def _dwconv_sample(state_t, u_t, w, b):
    ls, bs, d = u_t.shape
    tl = CONV_LANE_TILE
    return pl.pallas_call(
        _dwconv_sample_kernel,
        grid=(d // tl,),
        in_specs=[
            pl.BlockSpec((CONV_HALO, bs, tl), lambda j: (0, 0, j)),
            pl.BlockSpec((ls, bs, tl), lambda j: (0, 0, j)),
            pl.BlockSpec((CONV_TAPS, tl), lambda j: (0, j)),
            pl.BlockSpec((1, tl), lambda j: (0, j)),
        ],
        out_specs=pl.BlockSpec((ls, bs, tl), lambda j: (0, 0, j)),
        out_shape=jax.ShapeDtypeStruct((ls, bs, d), F32),
        compiler_params=_params("parallel"),
        name="dwconv_sample",
    )(state_t, u_t, w, b.reshape(1, d))


def _ln_pw2_kernel(y_ref, x_ref, lg_ref, lb_ref, w_ref, b_ref, o_ref, z_ref):
    @pl.when(pl.program_id(1) == 0)
    def _():
        y = y_ref[...]
        mu = jnp.mean(y, axis=-1, keepdims=True)
        yc = y - mu
        var = jnp.mean(yc * yc, axis=-1, keepdims=True)
        z_ref[...] = _silu(yc * lax.rsqrt(var + EPS) * lg_ref[...] + lb_ref[...])

    o_ref[...] = x_ref[...] + _dot(z_ref[...], w_ref[...]) + b_ref[...]


def _ln_pw2(y, x, lg, lb, w, b):
    t, d = x.shape
    dc = y.shape[1]
    tm, tn = TOKEN_TILE, PROJ_TILE
    return pl.pallas_call(
        _ln_pw2_kernel,
        grid=(t // tm, d // tn),
        in_specs=[
            pl.BlockSpec((tm, dc), lambda i, j: (i, 0)),
            pl.BlockSpec((tm, tn), lambda i, j: (i, j)),
            pl.BlockSpec((1, dc), lambda i, j: (0, 0)),
            pl.BlockSpec((1, dc), lambda i, j: (0, 0)),
            pl.BlockSpec((dc, tn), lambda i, j: (0, j)),
            pl.BlockSpec((1, tn), lambda i, j: (0, j)),
        ],
        out_specs=pl.BlockSpec((tm, tn), lambda i, j: (i, j)),
        out_shape=jax.ShapeDtypeStruct((t, d), F32),
        scratch_shapes=[pltpu.VMEM((tm, dc), F32)],
        compiler_params=_params("parallel", "arbitrary"),
        name="conv_ln_pw2",
    )(y, x, lg.reshape(1, dc), lb.reshape(1, dc), w, b.reshape(1, d))


def _log_sigmoid(z):
    return jnp.minimum(z, 0.0) - jnp.log1p(jnp.exp(-jnp.abs(z)))


def _gla_proj_kernel(x_ref, g_ref, w_ref, wg1_ref, wg2_ref, bg_ref, p_ref, la_ref, h_ref):
    @pl.when(pl.program_id(1) == 0)
    def _():
        h = _rms(x_ref[...], g_ref[...])
        h_ref[...] = h
        z = _dot(_dot(h, wg1_ref[...]), wg2_ref[...]) + bg_ref[...]
        la_ref[...] = _log_sigmoid(z) * (1.0 / GLA_TAU)

    p_ref[...] = _dot(h_ref[...], w_ref[...])


def _gla_proj(x, g, w_cat, wg1, wg2, bg):
    t, d = x.shape
    n = w_cat.shape[1]
    r, nk = wg2.shape
    tm, tn = TOKEN_TILE, PROJ_TILE
    return pl.pallas_call(
        _gla_proj_kernel,
        grid=(t // tm, n // tn),
        in_specs=[
            pl.BlockSpec((tm, d), lambda i, j: (i, 0)),
            pl.BlockSpec((1, d), lambda i, j: (0, 0)),
            pl.BlockSpec((d, tn), lambda i, j: (0, j)),
            pl.BlockSpec((d, r), lambda i, j: (0, 0)),
            pl.BlockSpec((r, nk), lambda i, j: (0, 0)),
            pl.BlockSpec((1, nk), lambda i, j: (0, 0)),
        ],
        out_specs=[
            pl.BlockSpec((tm, tn), lambda i, j: (i, j)),
            pl.BlockSpec((tm, nk), lambda i, j: (i, 0)),
        ],
        out_shape=[jax.ShapeDtypeStruct((t, n), F32), jax.ShapeDtypeStruct((t, nk), F32)],
        scratch_shapes=[pltpu.VMEM((tm, d), F32)],
        compiler_params=_params("parallel", "arbitrary"),
        name="gla_proj",
    )(x, g.reshape(1, d), w_cat, wg1, wg2, bg.reshape(1, nk))


def _bf16_limbs(x):
    hi = x.astype(jnp.bfloat16).astype(F32)
    mid = (x - hi).astype(jnp.bfloat16).astype(F32)
    return hi, mid, x - hi - mid


def _gla_gate(o, r, gn):
    return o * lax.rsqrt(jnp.mean(o * o, axis=-1, keepdims=True) + EPS) * gn * _silu(r)


def _gla_chunk(q, k, v, la, s0):
    c, dk = q.shape
    n_sub = c // GLA_SUB
    neg_inf = -jnp.inf
    row = lax.broadcasted_iota(jnp.int32, (c, c), 0)
    col = lax.broadcasted_iota(jnp.int32, (c, c), 1)
    tri = jnp.where(row >= col, 1.0, 0.0).astype(F32)
    b = _dot(jnp.concatenate([tri, tri, tri], axis=1), jnp.concatenate(_bf16_limbs(la), axis=0))

    o = _dot(q * jnp.exp(b), s0)

    row_c = lax.broadcasted_iota(jnp.int32, (c, 1), 0)
    lane_c = lax.broadcasted_iota(jnp.int32, (GLA_SUB, c), 1)
    t_sub = lax.broadcasted_iota(jnp.int32, (GLA_SUB, 1), 0)
    blocks = []
    for i in range(n_sub):
        lo = GLA_SUB * i
        qi, ki, bi = q[lo:lo + GLA_SUB], k[lo:lo + GLA_SUB], b[lo:lo + GLA_SUB]
        if i == 0:
            sc = jnp.zeros((GLA_SUB, c), F32)
        else:
            b_start = b[lo - 1:lo]
            qt = qi * jnp.exp(bi - b_start)
            kt = k * jnp.exp(jnp.where(row_c < lo, b_start - b, neg_inf))
            sc = lax.dot_general(qt, kt, (((1,), (1,)), ((), ())), preferred_element_type=F32)
        for s in range(GLA_SUB):
            dec = jnp.exp(jnp.where(t_sub >= s, bi - bi[s:s + 1], neg_inf))
            colv = jnp.sum(qi * dec * ki[s:s + 1], axis=-1, keepdims=True)
            sc = sc + jnp.where(lane_c == lo + s, colv, 0.0)
        blocks.append(sc)
    scores = blocks[0] if n_sub == 1 else jnp.concatenate(blocks, axis=0)
    o = o + _dot(scores, v)

    b_last = b[c - 1:c]
    kd = k * jnp.exp(b_last - b)
    upd = lax.dot_general(kd, v, (((0,), (0,)), ((), ())), preferred_element_type=F32)
    s_new = _column_scale(jnp.exp(b_last), s0) + upd
    return o, s_new


def _column_scale(d_row, s):
    dk, dv = s.shape
    d_col = jnp.transpose(jnp.broadcast_to(d_row, (128, dk)))
    return jnp.concatenate([d_col] * (dv // 128), axis=1) * s


def _gla_prompt_kernel(q_ref, k_ref, v_ref, r_ref, la_ref, gn_ref, og_ref, sn_ref, s_ref):
    seq_len, dk = q_ref.shape
    q_scale = dk ** -0.5
    s_ref[...] = jnp.zeros_like(s_ref)

    def run(r0, c):
        rows = pl.ds(r0, c)
        o, s_new = _gla_chunk(q_ref[rows, :] * q_scale, k_ref[rows, :], v_ref[rows, :], la_ref[rows, :],
                              s_ref[...])
        s_ref[...] = s_new
        og_ref[rows, :] = _gla_gate(o, r_ref[rows, :], gn_ref[...])

    run(0, N_META_ROWS)

    def body(j, carry):
        run(pl.multiple_of(N_META_ROWS + j * GLA_CHUNK, N_META_ROWS), GLA_CHUNK)
        return carry

    lax.fori_loop(0, (seq_len - N_META_ROWS) // GLA_CHUNK, body, 0)
    sn_ref[0, 0] = s_ref[...]


def _gla_prompt(p, la, gn, *, n_seq, seq_len, t_pad):
    h = GLA_HEADS
    dk = la.shape[1] // h
    dv = gn.shape[0]
    assert dv == 2 * dk and (seq_len - N_META_ROWS) % GLA_CHUNK == 0
    return pl.pallas_call(
        _gla_prompt_kernel,
        grid=(n_seq, h),
        in_specs=[
            pl.BlockSpec((seq_len, dk), lambda b, i: (b, i)),
            pl.BlockSpec((seq_len, dk), lambda b, i: (b, h + i)),
            pl.BlockSpec((seq_len, dv), lambda b, i: (b, h + i)),
            pl.BlockSpec((seq_len, dv), lambda b, i: (b, 2 * h + i)),
            pl.BlockSpec((seq_len, dk), lambda b, i: (b, i)),
            pl.BlockSpec((1, dv), lambda b, i: (0, 0)),
        ],
        out_specs=[
            pl.BlockSpec((seq_len, dv), lambda b, i: (b, i)),
            pl.BlockSpec((1, 1, dk, dv), lambda b, i: (b, i, 0, 0)),
        ],
        out_shape=[jax.ShapeDtypeStruct((t_pad, h * dv), F32),
                   jax.ShapeDtypeStruct((n_seq, h, dk, dv), F32)],
        scratch_shapes=[pltpu.VMEM((dk, dv), F32)],
        compiler_params=_params("parallel", "parallel"),
        name="gla_prompt",
    )(p, p, p, p, la, gn.reshape(1, dv))


def _gla_sample_kernel(p_ref, la_ref, s_ref, gn_ref, og_ref, sn_ref, *, seq_len):
    rows = p_ref.shape[0]
    n_seq = rows // seq_len
    dk = la_ref.shape[1] // GLA_HEADS
    dv = 2 * dk
    q_scale = dk ** -0.5
    neg_inf = -jnp.inf
    t_row = lax.broadcasted_iota(jnp.int32, (rows, 1), 0)
    for hd in range(GLA_HEADS):
        q = p_ref[:, hd * dk:(hd + 1) * dk] * q_scale
        k = p_ref[:, (GLA_HEADS + hd) * dk:(GLA_HEADS + hd + 1) * dk]
        v = p_ref[:, (GLA_HEADS + hd) * dv:(GLA_HEADS + hd + 1) * dv]
        r = p_ref[:, (2 * GLA_HEADS + hd) * dv:(2 * GLA_HEADS + hd + 1) * dv]
        la = la_ref[:, hd * dk:(hd + 1) * dk]

        def in_seq_from(s):
            return (t_row >= s) & (t_row < (s // seq_len + 1) * seq_len)

        b = jnp.zeros_like(la)
        for s in range(rows):
            b = b + jnp.where(in_seq_from(s), la[s:s + 1], 0.0)

        qe = q * jnp.exp(b)
        o = jnp.zeros((rows, dv), F32)
        for g in range(n_seq):
            mine = (t_row >= g * seq_len) & (t_row < (g + 1) * seq_len)
            o = jnp.where(mine, _dot(qe, s_ref[g, hd]), o)
        for s in range(rows):
            dec = jnp.exp(jnp.where(in_seq_from(s), b - b[s:s + 1], neg_inf))
            colv = jnp.sum(q * dec * k[s:s + 1], axis=-1, keepdims=True)
            o = o + colv * v[s:s + 1]
        og_ref[:, hd * dv:(hd + 1) * dv] = _gla_gate(o, r, gn_ref[...])

        for g in range(n_seq):
            last = (g + 1) * seq_len - 1
            b_last = b[last:last + 1]
            mine = (t_row >= g * seq_len) & (t_row < (g + 1) * seq_len)
            kd = k * jnp.exp(jnp.where(mine, b_last - b, neg_inf))
            upd = lax.dot_general(kd, v, (((0,), (0,)), ((), ())), preferred_element_type=F32)
            sn_ref[g, hd] = _column_scale(jnp.exp(b_last), s_ref[g, hd]) + upd


def _gla_sample(p_s, la_s, state, gn, *, seq_len):
    rows = 8
    n, h, dk, dv = state.shape
    t = p_s.shape[0]
    per = rows // seq_len
    return pl.pallas_call(
        functools.partial(_gla_sample_kernel, seq_len=seq_len),
        grid=(t // rows,),
        in_specs=[
            pl.BlockSpec((rows, p_s.shape[1]), lambda i: (i, 0)),
            pl.BlockSpec((rows, la_s.shape[1]), lambda i: (i, 0)),
            pl.BlockSpec((per, h, dk, dv), lambda i: (i, 0, 0, 0)),
            pl.BlockSpec((1, dv), lambda i: (0, 0)),
        ],
        out_specs=[
            pl.BlockSpec((rows, h * dv), lambda i: (i, 0)),
            pl.BlockSpec((per, h, dk, dv), lambda i: (i, 0, 0, 0)),
        ],
        out_shape=[jax.ShapeDtypeStruct((t, h * dv), F32), jax.ShapeDtypeStruct(state.shape, F32)],
        compiler_params=_params("parallel"),
        name="gla_sample",
    )(p_s, la_s, state, gn.reshape(1, dv))


def _out_proj_kernel(a_ref, x_ref, w_ref, o_ref):
    o_ref[...] = x_ref[...] + _dot(a_ref[...], w_ref[...])


def _out_proj(a, x, w):
    t, d = x.shape
    kdim = a.shape[1]
    tm, tn = TOKEN_TILE, PROJ_TILE
    return pl.pallas_call(
        _out_proj_kernel,
        grid=(t // tm, d // tn),
        in_specs=[
            pl.BlockSpec((tm, kdim), lambda i, j: (i, 0)),
            pl.BlockSpec((tm, tn), lambda i, j: (i, j)),
            pl.BlockSpec((kdim, tn), lambda i, j: (0, j)),
        ],
        out_specs=pl.BlockSpec((tm, tn), lambda i, j: (i, j)),
        out_shape=jax.ShapeDtypeStruct((t, d), F32),
        compiler_params=_params("parallel", "arbitrary"),
        name="gla_out_proj",
    )(a, x, w)


def kernel(x_prompt, x_sample, state_conv, state_gla, meta_tokens, norm_ffn, w_ffn_gate, w_ffn_up, w_ffn_down,
           norm_mix, conv_w_pw1, conv_b_pw1, conv_w_dw, conv_b_dw, conv_ln_g, conv_ln_b, conv_w_pw2, conv_b_pw2,
           gla_w_q, gla_w_k, gla_w_v, gla_w_g1, gla_w_g2, gla_b_g, gla_w_r, gla_gn_g, gla_w_o, norm_final):
    bp, sp, d = x_prompt.shape
    bs, ls, _ = x_sample.shape
    lp = N_META_ROWS + sp
    tp, ts = bp * lp, bs * ls
    t_pad = -(-(tp + ts) // TOKEN_TILE) * TOKEN_TILE
    assert 8 % ls == 0 and ts % 8 == 0 and tp % 8 == 0

    meta = jnp.broadcast_to(meta_tokens[None], (bp, N_META_ROWS, d))
    x = jnp.concatenate([
        jnp.concatenate([meta, x_prompt], axis=1).reshape(tp, d),
        jnp.transpose(x_sample, (1, 0, 2)).reshape(ts, d),
        jnp.zeros((t_pad - tp - ts, d), F32)], axis=0)

    def ffn(x, i, j, final_norm=False):
        return _ffn(x, norm_ffn[i, j], w_ffn_gate[i, j], w_ffn_up[i, j], w_ffn_down[i, j], norm_final,
                    final_norm=final_norm)

    x = ffn(x, 0, 0)
    u = _pw1_glu(x, norm_mix[0], conv_w_pw1[0], conv_b_pw1[0])
    u_s = u[tp:tp + ts].reshape(ls, bs, d)
    y_p = _dwconv_prompt(u, conv_w_dw[0], conv_b_dw[0], n_seq=bp, seq_len=lp)
    y_s = _dwconv_sample(jnp.transpose(state_conv[0], (1, 0, 2)), u_s, conv_w_dw[0], conv_b_dw[0])
    y = jnp.concatenate([y_p, y_s.reshape(ts, d), jnp.zeros((t_pad - tp - ts, d), F32)], axis=0)
    x = _ln_pw2(y, x, conv_ln_g[0], conv_ln_b[0], conv_w_pw2[0], conv_b_pw2[0])
    x = ffn(x, 0, 1)
    new_conv_prompt = u[:tp].reshape(bp, lp, d)[None, :, lp - CONV_HALO:]
    new_conv_sample = jnp.concatenate([state_conv[0], jnp.transpose(u_s, (1, 0, 2))], axis=1)[None, :, ls:]

    x = ffn(x, 1, 0)
    w_cat = jnp.concatenate([gla_w_q[0], gla_w_k[0], gla_w_v[0], gla_w_r[0]], axis=1)
    p, la = _gla_proj(x, norm_mix[1], w_cat, gla_w_g1[0], gla_w_g2[0], gla_b_g[0])
    og, new_gla_prompt = _gla_prompt(p, la, gla_gn_g[0], n_seq=bp, seq_len=lp, t_pad=t_pad)

    def seq_major(a):
        return jnp.transpose(a[tp:tp + ts].reshape(ls, bs, -1), (1, 0, 2)).reshape(ts, -1)

    og_s, new_gla_sample = _gla_sample(seq_major(p), seq_major(la), state_gla[0], gla_gn_g[0], seq_len=ls)
    og_s = jnp.transpose(og_s.reshape(bs, ls, -1), (1, 0, 2)).reshape(ts, -1)
    og = lax.dynamic_update_slice(
        og, jnp.concatenate([og_s, jnp.zeros((t_pad - tp - ts, og.shape[1]), F32)], axis=0), (tp, 0))
    x = _out_proj(og, x, gla_w_o[0])
    x = ffn(x, 1, 1, final_norm=True)

    y_prompt = x[:tp].reshape(bp, lp, d)[:, N_META_ROWS:]
    y_sample = jnp.transpose(x[tp:tp + ts].reshape(ls, bs, d), (1, 0, 2))
    return (y_prompt, y_sample, new_conv_prompt, new_gla_prompt[None], new_conv_sample, new_gla_sample[None])
```

```python
import functools

import jax
import jax.numpy as jnp
from jax import lax
from jax.experimental import pallas as pl
from jax.experimental.pallas import tpu as pltpu

F32 = jnp.float32
EPS = 1e-6
N_META_ROWS = 16
CONV_TAPS = 31
CONV_HALO = CONV_TAPS - 1
HALO_PAD = 32
GLA_HEADS = 4
GLA_SUB = 16
GLA_CHUNK = 64
GLA_TAU = 16.0
V7X_VMEM_LIMIT = 60000 * 1024

MAX_TOKEN_TILE = 1152
FFN_TILE = 256
PROJ_TILE = 512
CONV_LANE_TILE = 256


def _token_blocks(t):
    return -(-t // MAX_TOKEN_TILE)


def _token_tile(t_pad):
    nb = _token_blocks(t_pad)
    assert t_pad % (8 * nb) == 0, t_pad
    return t_pad // nb


def _params(*sem):
    return pltpu.CompilerParams(dimension_semantics=sem, vmem_limit_bytes=V7X_VMEM_LIMIT)


def _rms(x, g):
    return x * lax.rsqrt(jnp.mean(x * x, axis=-1, keepdims=True) + EPS) * g


def _silu(x):
    return x * jax.nn.sigmoid(x)


def _dot(a, b):
    return jnp.dot(a, b, preferred_element_type=F32)


def _ffn_kernel(x_ref, g_ref, wg_ref, wu_ref, wd_ref, gf_ref, o_ref, h_ref, *, final_norm):
    f = pl.program_id(1)

    @pl.when(f == 0)
    def _():
        x = x_ref[...]
        h_ref[...] = _rms(x, g_ref[...])
        o_ref[...] = x

    h = h_ref[...]
    a = 0.5 * _silu(_dot(h, wg_ref[...])) * _dot(h, wu_ref[...])
    o_ref[...] += _dot(a, wd_ref[...])

    if final_norm:
        @pl.when(f == pl.num_programs(1) - 1)
        def _():
            o_ref[...] = _rms(o_ref[...], gf_ref[...])


def _ffn(x, g, wg, wu, wd, gf, *, layer, slot, final_norm):
    t, d = x.shape
    f = wg.shape[-1]
    tm, tf = _token_tile(t), FFN_TILE
    return pl.pallas_call(
        functools.partial(_ffn_kernel, final_norm=final_norm),
        grid=(t // tm, f // tf),
        in_specs=[
            pl.BlockSpec((tm, d), lambda i, j: (i, 0)),
            pl.BlockSpec((None, None, 1, d), lambda i, j: (layer, slot, 0, 0)),
            pl.BlockSpec((None, None, d, tf), lambda i, j: (layer, slot, 0, j)),
            pl.BlockSpec((None, None, d, tf), lambda i, j: (layer, slot, 0, j)),
            pl.BlockSpec((None, None, tf, d), lambda i, j: (layer, slot, j, 0)),
            pl.BlockSpec((1, d), lambda i, j: (0, 0)),
        ],
        out_specs=pl.BlockSpec((tm, d), lambda i, j: (i, 0)),
        out_shape=jax.ShapeDtypeStruct((t, d), F32),
        scratch_shapes=[pltpu.VMEM((tm, d), F32)],
        compiler_params=_params("parallel", "arbitrary"),
        name="ffn_final" if final_norm else "ffn",
    )(x, g.reshape(g.shape[0], g.shape[1], 1, d), wg, wu, wd, gf.reshape(1, d))


def _pw1_kernel(x_ref, g_ref, wa_ref, wb_ref, ba_ref, bb_ref, u_ref, h_ref):
    @pl.when(pl.program_id(1) == 0)
    def _():
        h_ref[...] = _rms(x_ref[...], g_ref[...])

    h = h_ref[...]
    a = _dot(h, wa_ref[...]) + ba_ref[...]
    b = _dot(h, wb_ref[...]) + bb_ref[...]
    u_ref[...] = a * jax.nn.sigmoid(b)


def _pw1_glu(x, g, w, b):
    t, d = x.shape
    dc = w.shape[1] // 2
    tm, tn = _token_tile(t), PROJ_TILE
    nc = dc // tn
    b2 = b.reshape(1, 2 * dc)
    return pl.pallas_call(
        _pw1_kernel,
        grid=(t // tm, nc),
        in_specs=[
            pl.BlockSpec((tm, d), lambda i, j: (i, 0)),
            pl.BlockSpec((1, d), lambda i, j: (0, 0)),
            pl.BlockSpec((d, tn), lambda i, j: (0, j)),
            pl.BlockSpec((d, tn), lambda i, j: (0, j + nc)),
            pl.BlockSpec((1, tn), lambda i, j: (0, j)),
            pl.BlockSpec((1, tn), lambda i, j: (0, j + nc)),
        ],
        out_specs=pl.BlockSpec((tm, tn), lambda i, j: (i, j)),
        out_shape=jax.ShapeDtypeStruct((t, dc), F32),
        scratch_shapes=[pltpu.VMEM((tm, d), F32)],
        compiler_params=_params("parallel", "arbitrary"),
        name="conv_pw1_glu",
    )(x, g.reshape(1, d), w, w, b2, b2)


def _conv_window(win, w_ref, b_ref, rc):
    lead = HALO_PAD - CONV_HALO
    acc = jnp.broadcast_to(b_ref[...], (rc, win.shape[1]))
    for s in range(8):
        part = None
        for a in range((CONV_TAPS + lead + 7) // 8):
            k = 8 * a + s - lead
            if 0 <= k < CONV_TAPS:
                term = w_ref[k:k + 1, :] * win[8 * a:8 * a + rc + 8]
                part = term if part is None else part + term
        acc = acc + part[s:s + rc]
    return acc


def _dwconv_prompt_kernel(cur_ref, w_ref, b_ref, y_ref, ext_ref, *, rc):
    lp, tl = cur_ref.shape
    ext_ref[0:HALO_PAD, :] = jnp.zeros((HALO_PAD, tl), F32)
    ext_ref[HALO_PAD:HALO_PAD + lp, :] = cur_ref[...]
    ext_ref[HALO_PAD + lp:HALO_PAD + lp + 8, :] = jnp.zeros((8, tl), F32)

    def body(c, carry):
        r0 = pl.multiple_of(c * rc, 8)
        y_ref[pl.ds(r0, rc), :] = _conv_window(ext_ref[pl.ds(r0, rc + HALO_PAD + 8), :], w_ref, b_ref, rc)
        return carry

    lax.fori_loop(0, lp // rc, body, 0)


def _dwconv_prompt(u, w, b, *, n_seq, seq_len):
    d = u.shape[1]
    tl = CONV_LANE_TILE
    rc = max(r for r in range(8, 65, 8) if seq_len % r == 0)
    return pl.pallas_call(
        functools.partial(_dwconv_prompt_kernel, rc=rc),
        grid=(n_seq, d // tl),
        in_specs=[
            pl.BlockSpec((seq_len, tl), lambda i, j: (i, j)),
            pl.BlockSpec((CONV_TAPS, tl), lambda i, j: (0, j)),
            pl.BlockSpec((1, tl), lambda i, j: (0, j)),
        ],
        out_specs=pl.BlockSpec((seq_len, tl), lambda i, j: (i, j)),
        out_shape=jax.ShapeDtypeStruct((n_seq * seq_len, d), F32),
        scratch_shapes=[pltpu.VMEM((HALO_PAD + seq_len + 8, tl), F32)],
        compiler_params=_params("parallel", "parallel"),
        name="dwconv_prompt",
    )(u, w, b.reshape(1, d))


def _dwconv_sample_kernel(st_ref, u_ref, w_ref, b_ref, y_ref):
    ls, bs, tl = u_ref.shape
    for t in range(ls):
        acc = jnp.broadcast_to(b_ref[...], (bs, tl))
        for k in range(CONV_TAPS):
            j = t + k
            row = st_ref[j] if j < CONV_HALO else u_ref[j - CONV_HALO]
            acc = acc + w_ref[k:k + 1, :] * row
        y_ref[t] = acc


def _dwconv_sample(state_t, u_t, w, b):
    ls, bs, d = u_t.shape
    tl = CONV_LANE_TILE
    return pl.pallas_call(
        _dwconv_sample_kernel,
        grid=(d // tl,),
        in_specs=[
            pl.BlockSpec((CONV_HALO, bs, tl), lambda j: (0, 0, j)),
            pl.BlockSpec((ls, bs, tl), lambda j: (0, 0, j)),
            pl.BlockSpec((CONV_TAPS, tl), lambda j: (0, j)),
            pl.BlockSpec((1, tl), lambda j: (0, j)),
        ],
        out_specs=pl.BlockSpec((ls, bs, tl), lambda j: (0, 0, j)),
        out_shape=jax.ShapeDtypeStruct((ls, bs, d), F32),
        compiler_params=_params("parallel"),
        name="dwconv_sample",
    )(state_t, u_t, w, b.reshape(1, d))


def _ln_pw2_kernel(y_ref, x_ref, lg_ref, lb_ref, w_ref, b_ref, o_ref, z_ref):
    @pl.when(pl.program_id(1) == 0)
    def _():
        y = y_ref[...]
        mu = jnp.mean(y, axis=-1, keepdims=True)
        yc = y - mu
        var = jnp.mean(yc * yc, axis=-1, keepdims=True)
        z_ref[...] = _silu(yc * lax.rsqrt(var + EPS) * lg_ref[...] + lb_ref[...])

    o_ref[...] = x_ref[...] + _dot(z_ref[...], w_ref[...]) + b_ref[...]


def _ln_pw2(y, x, lg, lb, w, b):
    t, d = x.shape
    dc = y.shape[1]
    tm, tn = _token_tile(t), PROJ_TILE
    return pl.pallas_call(
        _ln_pw2_kernel,
        grid=(t // tm, d // tn),
        in_specs=[
            pl.BlockSpec((tm, dc), lambda i, j: (i, 0)),
            pl.BlockSpec((tm, tn), lambda i, j: (i, j)),
            pl.BlockSpec((1, dc), lambda i, j: (0, 0)),
            pl.BlockSpec((1, dc), lambda i, j: (0, 0)),
            pl.BlockSpec((dc, tn), lambda i, j: (0, j)),
            pl.BlockSpec((1, tn), lambda i, j: (0, j)),
        ],
        out_specs=pl.BlockSpec((tm, tn), lambda i, j: (i, j)),
        out_shape=jax.ShapeDtypeStruct((t, d), F32),
        scratch_shapes=[pltpu.VMEM((tm, dc), F32)],
        compiler_params=_params("parallel", "arbitrary"),
        name="conv_ln_pw2",
    )(y, x, lg.reshape(1, dc), lb.reshape(1, dc), w, b.reshape(1, d))


def _log_sigmoid(z):
    return jnp.minimum(z, 0.0) - jnp.log1p(jnp.exp(-jnp.abs(z)))


def _gla_proj_kernel(x_ref, g_ref, w_ref, wg1_ref, wg2_ref, bg_ref, p_ref, la_ref, h_ref):
    @pl.when(pl.program_id(1) == 0)
    def _():
        h = _rms(x_ref[...], g_ref[...])
        h_ref[...] = h
        z = _dot(_dot(h, wg1_ref[...]), wg2_ref[...]) + bg_ref[...]
        la_ref[...] = _log_sigmoid(z) * (1.0 / GLA_TAU)

    p_ref[...] = _dot(h_ref[...], w_ref[...])


def _gla_proj(x, g, w_cat, wg1, wg2, bg):
    t, d = x.shape
    n = w_cat.shape[1]
    r, nk = wg2.shape
    tm, tn = _token_tile(t), PROJ_TILE
    return pl.pallas_call(
        _gla_proj_kernel,
        grid=(t // tm, n // tn),
        in_specs=[
            pl.BlockSpec((tm, d), lambda i, j: (i, 0)),
            pl.BlockSpec((1, d), lambda i, j: (0, 0)),
            pl.BlockSpec((d, tn), lambda i, j: (0, j)),
            pl.BlockSpec((d, r), lambda i, j: (0, 0)),
            pl.BlockSpec((r, nk), lambda i, j: (0, 0)),
            pl.BlockSpec((1, nk), lambda i, j: (0, 0)),
        ],
        out_specs=[
            pl.BlockSpec((tm, tn), lambda i, j: (i, j)),
            pl.BlockSpec((tm, nk), lambda i, j: (i, 0)),
        ],
        out_shape=[jax.ShapeDtypeStruct((t, n), F32), jax.ShapeDtypeStruct((t, nk), F32)],
        scratch_shapes=[pltpu.VMEM((tm, d), F32)],
        compiler_params=_params("parallel", "arbitrary"),
        name="gla_proj",
    )(x, g.reshape(1, d), w_cat, wg1, wg2, bg.reshape(1, nk))


def _bf16_limbs(x):
    hi = x.astype(jnp.bfloat16).astype(F32)
    mid = (x - hi).astype(jnp.bfloat16).astype(F32)
    return hi, mid, x - hi - mid


def _gla_gate(o, r, gn):
    return o * lax.rsqrt(jnp.mean(o * o, axis=-1, keepdims=True) + EPS) * gn * _silu(r)


def _gla_chunk(q, k, v, la, s0):
    c, dk = q.shape
    n_sub = c // GLA_SUB
    neg_inf = -jnp.inf
    row = lax.broadcasted_iota(jnp.int32, (c, c), 0)
    col = lax.broadcasted_iota(jnp.int32, (c, c), 1)
    tri = jnp.where(row >= col, 1.0, 0.0).astype(F32)
    b = _dot(jnp.concatenate([tri, tri, tri], axis=1), jnp.concatenate(_bf16_limbs(la), axis=0))

    o = _dot(q * jnp.exp(b), s0)

    row_c = lax.broadcasted_iota(jnp.int32, (c, 1), 0)
    lane_c = lax.broadcasted_iota(jnp.int32, (GLA_SUB, c), 1)
    t_sub = lax.broadcasted_iota(jnp.int32, (GLA_SUB, 1), 0)
    blocks = []
    for i in range(n_sub):
        lo = GLA_SUB * i
        qi, ki, bi = q[lo:lo + GLA_SUB], k[lo:lo + GLA_SUB], b[lo:lo + GLA_SUB]
        if i == 0:
            sc = jnp.zeros((GLA_SUB, c), F32)
        else:
            b_start = b[lo - 1:lo]
            qt = qi * jnp.exp(bi - b_start)
            kt = k * jnp.exp(jnp.where(row_c < lo, b_start - b, neg_inf))
            sc = lax.dot_general(qt, kt, (((1,), (1,)), ((), ())), preferred_element_type=F32)
        for s in range(GLA_SUB):
            dec = jnp.exp(jnp.where(t_sub >= s, bi - bi[s:s + 1], neg_inf))
            colv = jnp.sum(qi * dec * ki[s:s + 1], axis=-1, keepdims=True)
            sc = sc + jnp.where(lane_c == lo + s, colv, 0.0)
        blocks.append(sc)
    scores = blocks[0] if n_sub == 1 else jnp.concatenate(blocks, axis=0)
    o = o + _dot(scores, v)

    b_last = b[c - 1:c]
    kd = k * jnp.exp(b_last - b)
    upd = lax.dot_general(kd, v, (((0,), (0,)), ((), ())), preferred_element_type=F32)
    s_new = _column_scale(jnp.exp(b_last), s0) + upd
    return o, s_new


def _column_scale(d_row, s):
    dk, dv = s.shape
    d_col = jnp.transpose(jnp.broadcast_to(d_row, (128, dk)))
    return jnp.concatenate([d_col] * (dv // 128), axis=1) * s


def _gla_prompt_kernel(q_ref, k_ref, v_ref, r_ref, la_ref, gn_ref, og_ref, sn_ref, s_ref):
    seq_len, dk = q_ref.shape
    q_scale = dk ** -0.5
    s_ref[...] = jnp.zeros_like(s_ref)

    def run(r0, c):
        rows = pl.ds(r0, c)
        o, s_new = _gla_chunk(q_ref[rows, :] * q_scale, k_ref[rows, :], v_ref[rows, :], la_ref[rows, :],
                              s_ref[...])
        s_ref[...] = s_new
        og_ref[rows, :] = _gla_gate(o, r_ref[rows, :], gn_ref[...])

    run(0, N_META_ROWS)

    def body(j, carry):
        run(pl.multiple_of(N_META_ROWS + j * GLA_CHUNK, N_META_ROWS), GLA_CHUNK)
        return carry

    lax.fori_loop(0, (seq_len - N_META_ROWS) // GLA_CHUNK, body, 0)
    sn_ref[0, 0] = s_ref[...]


def _gla_prompt(p, la, gn, *, n_seq, seq_len, t_pad):
    h = GLA_HEADS
    dk = la.shape[1] // h
    dv = gn.shape[0]
    assert dv == 2 * dk and (seq_len - N_META_ROWS) % GLA_CHUNK == 0
    return pl.pallas_call(
        _gla_prompt_kernel,
        grid=(n_seq, h),
        in_specs=[
            pl.BlockSpec((seq_len, dk), lambda b, i: (b, i)),
            pl.BlockSpec((seq_len, dk), lambda b, i: (b, h + i)),
            pl.BlockSpec((seq_len, dv), lambda b, i: (b, h + i)),
            pl.BlockSpec((seq_len, dv), lambda b, i: (b, 2 * h + i)),
            pl.BlockSpec((seq_len, dk), lambda b, i: (b, i)),
            pl.BlockSpec((1, dv), lambda b, i: (0, 0)),
        ],
        out_specs=[
            pl.BlockSpec((seq_len, dv), lambda b, i: (b, i)),
            pl.BlockSpec((1, 1, dk, dv), lambda b, i: (b, i, 0, 0)),
        ],
        out_shape=[jax.ShapeDtypeStruct((t_pad, h * dv), F32),
                   jax.ShapeDtypeStruct((n_seq, h, dk, dv), F32)],
        scratch_shapes=[pltpu.VMEM((dk, dv), F32)],
        compiler_params=_params("parallel", "parallel"),
        name="gla_prompt",
    )(p, p, p, p, la, gn.reshape(1, dv))


def _gla_sample_kernel(p_ref, la_ref, s_ref, gn_ref, og_ref, sn_ref, *, seq_len):
    rows = p_ref.shape[0]
    n_seq = rows // seq_len
    dk = la_ref.shape[1] // GLA_HEADS
    dv = 2 * dk
    q_scale = dk ** -0.5
    neg_inf = -jnp.inf
    t_row = lax.broadcasted_iota(jnp.int32, (rows, 1), 0)
    for hd in range(GLA_HEADS):
        q = p_ref[:, hd * dk:(hd + 1) * dk] * q_scale
        k = p_ref[:, (GLA_HEADS + hd) * dk:(GLA_HEADS + hd + 1) * dk]
        v = p_ref[:, (GLA_HEADS + hd) * dv:(GLA_HEADS + hd + 1) * dv]
        r = p_ref[:, (2 * GLA_HEADS + hd) * dv:(2 * GLA_HEADS + hd + 1) * dv]
        la = la_ref[:, hd * dk:(hd + 1) * dk]

        def in_seq_from(s):
            return (t_row >= s) & (t_row < (s // seq_len + 1) * seq_len)

        b = jnp.zeros_like(la)
        for s in range(rows):
            b = b + jnp.where(in_seq_from(s), la[s:s + 1], 0.0)

        qe = q * jnp.exp(b)
        o = jnp.zeros((rows, dv), F32)
        for g in range(n_seq):
            mine = (t_row >= g * seq_len) & (t_row < (g + 1) * seq_len)
            o = jnp.where(mine, _dot(qe, s_ref[g, hd]), o)
        for s in range(rows):
            dec = jnp.exp(jnp.where(in_seq_from(s), b - b[s:s + 1], neg_inf))
            colv = jnp.sum(q * dec * k[s:s + 1], axis=-1, keepdims=True)
            o = o + colv * v[s:s + 1]
        og_ref[:, hd * dv:(hd + 1) * dv] = _gla_gate(o, r, gn_ref[...])

        for g in range(n_seq):
            last = (g + 1) * seq_len - 1
            b_last = b[last:last + 1]
            mine = (t_row >= g * seq_len) & (t_row < (g + 1) * seq_len)
            kd = k * jnp.exp(jnp.where(mine, b_last - b, neg_inf))
            upd = lax.dot_general(kd, v, (((0,), (0,)), ((), ())), preferred_element_type=F32)
            sn_ref[g, hd] = _column_scale(jnp.exp(b_last), s_ref[g, hd]) + upd


def _gla_sample(p_s, la_s, state, gn, *, seq_len):
    rows = 8
    n, h, dk, dv = state.shape
    t = p_s.shape[0]
    per = rows // seq_len
    return pl.pallas_call(
        functools.partial(_gla_sample_kernel, seq_len=seq_len),
        grid=(t // rows,),
        in_specs=[
            pl.BlockSpec((rows, p_s.shape[1]), lambda i: (i, 0)),
            pl.BlockSpec((rows, la_s.shape[1]), lambda i: (i, 0)),
            pl.BlockSpec((per, h, dk, dv), lambda i: (i, 0, 0, 0)),
            pl.BlockSpec((1, dv), lambda i: (0, 0)),
        ],
        out_specs=[
            pl.BlockSpec((rows, h * dv), lambda i: (i, 0)),
            pl.BlockSpec((per, h, dk, dv), lambda i: (i, 0, 0, 0)),
        ],
        out_shape=[jax.ShapeDtypeStruct((t, h * dv), F32), jax.ShapeDtypeStruct(state.shape, F32)],
        compiler_params=_params("parallel"),
        name="gla_sample",
    )(p_s, la_s, state, gn.reshape(1, dv))


def _out_proj_kernel(a_ref, x_ref, w_ref, o_ref):
    o_ref[...] = x_ref[...] + _dot(a_ref[...], w_ref[...])


def _out_proj(a, x, w):
    t, d = x.shape
    kdim = a.shape[1]
    tm, tn = _token_tile(t), PROJ_TILE
    return pl.pallas_call(
        _out_proj_kernel,
        grid=(t // tm, d // tn),
        in_specs=[
            pl.BlockSpec((tm, kdim), lambda i, j: (i, 0)),
            pl.BlockSpec((tm, tn), lambda i, j: (i, j)),
            pl.BlockSpec((kdim, tn), lambda i, j: (0, j)),
        ],
        out_specs=pl.BlockSpec((tm, tn), lambda i, j: (i, j)),
        out_shape=jax.ShapeDtypeStruct((t, d), F32),
        compiler_params=_params("parallel", "arbitrary"),
        name="gla_out_proj",
    )(a, x, w)


def kernel(x_prompt, x_sample, state_conv, state_gla, meta_tokens, norm_ffn, w_ffn_gate, w_ffn_up, w_ffn_down,
           norm_mix, conv_w_pw1, conv_b_pw1, conv_w_dw, conv_b_dw, conv_ln_g, conv_ln_b, conv_w_pw2, conv_b_pw2,
           gla_w_q, gla_w_k, gla_w_v, gla_w_g1, gla_w_g2, gla_b_g, gla_w_r, gla_gn_g, gla_w_o, norm_final):
    bp, sp, d = x_prompt.shape
    bs, ls, _ = x_sample.shape
    lp = N_META_ROWS + sp
    tp, ts = bp * lp, bs * ls
    row_quantum = 8 * _token_blocks(tp + ts)
    t_pad = -(-(tp + ts) // row_quantum) * row_quantum
    assert 8 % ls == 0 and ts % 8 == 0 and tp % 8 == 0

    meta = jnp.broadcast_to(meta_tokens[None], (bp, N_META_ROWS, d))
    x = jnp.concatenate([
        jnp.concatenate([meta, x_prompt], axis=1).reshape(tp, d),
        jnp.transpose(x_sample, (1, 0, 2)).reshape(ts, d),
        jnp.zeros((t_pad - tp - ts, d), F32)], axis=0)

    def ffn(x, i, j, final_norm=False):
        return _ffn(x, norm_ffn, w_ffn_gate, w_ffn_up, w_ffn_down, norm_final,
                    layer=i, slot=j, final_norm=final_norm)

    x = ffn(x, 0, 0)
    u = _pw1_glu(x, norm_mix[0], conv_w_pw1[0], conv_b_pw1[0])
    u_s = u[tp:tp + ts].reshape(ls, bs, d)
    y_p = _dwconv_prompt(u, conv_w_dw[0], conv_b_dw[0], n_seq=bp, seq_len=lp)
    y_s = _dwconv_sample(jnp.transpose(state_conv[0], (1, 0, 2)), u_s, conv_w_dw[0], conv_b_dw[0])
    y = jnp.concatenate([y_p, y_s.reshape(ts, d), jnp.zeros((t_pad - tp - ts, d), F32)], axis=0)
    x = _ln_pw2(y, x, conv_ln_g[0], conv_ln_b[0], conv_w_pw2[0], conv_b_pw2[0])
    x = ffn(x, 0, 1)
    new_conv_prompt = u[:tp].reshape(bp, lp, d)[None, :, lp - CONV_HALO:]
    new_conv_sample = jnp.concatenate([state_conv[0], jnp.transpose(u_s, (1, 0, 2))], axis=1)[None, :, ls:]

    x = ffn(x, 1, 0)
    w_cat = jnp.concatenate([gla_w_q[0], gla_w_k[0], gla_w_v[0], gla_w_r[0]], axis=1)
    p, la = _gla_proj(x, norm_mix[1], w_cat, gla_w_g1[0], gla_w_g2[0], gla_b_g[0])
    og, new_gla_prompt = _gla_prompt(p, la, gla_gn_g[0], n_seq=bp, seq_len=lp, t_pad=t_pad)

    def seq_major(a):
        return jnp.transpose(a[tp:tp + ts].reshape(ls, bs, -1), (1, 0, 2)).reshape(ts, -1)

    og_s, new_gla_sample = _gla_sample(seq_major(p), seq_major(la), state_gla[0], gla_gn_g[0], seq_len=ls)
    og_s = jnp.transpose(og_s.reshape(bs, ls, -1), (1, 0, 2)).reshape(ts, -1)
    og = lax.dynamic_update_slice(
        og, jnp.concatenate([og_s, jnp.zeros((t_pad - tp - ts, og.shape[1]), F32)], axis=0), (tp, 0))
    x = _out_proj(og, x, gla_w_o[0])
    x = ffn(x, 1, 1, final_norm=True)

    y_prompt = x[:tp].reshape(bp, lp, d)[:, N_META_ROWS:]
    y_sample = jnp.transpose(x[tp:tp + ts].reshape(ls, bs, d), (1, 0, 2))
    return (y_prompt, y_sample, new_conv_prompt, new_gla_prompt[None], new_conv_sample, new_gla_sample[None])
```

```python
import functools
import math

import jax
import jax.numpy as jnp
from jax import lax
from jax.experimental import pallas as pl
from jax.experimental.pallas import tpu as pltpu

F32 = jnp.float32
EPS = 1e-6
N_META_ROWS = 16
CONV_TAPS = 31
CONV_HALO = CONV_TAPS - 1
HALO_PAD = 32
GLA_HEADS = 4
GLA_HEADS_PER_STEP = 2
GLA_SUB = 16
GLA_CHUNK = 64
GLA_TAU = 16.0
LOG2_E = 1.4426950408889634
V7X_VMEM_LIMIT = 60000 * 1024

MAX_TOKEN_TILE = 1152
FFN_TILE = 256
PROJ_TILE = 512
CONV_LANE_TILE = 256


def _token_blocks(t):
    return -(-t // MAX_TOKEN_TILE)


def _token_tile(t_pad):
    nb = _token_blocks(t_pad)
    assert t_pad % (8 * nb) == 0, t_pad
    return t_pad // nb


def _params(*sem):
    return pltpu.CompilerParams(dimension_semantics=sem, vmem_limit_bytes=V7X_VMEM_LIMIT)


def _rms(x, g):
    return x * lax.rsqrt(jnp.mean(x * x, axis=-1, keepdims=True) + EPS) * g


def _silu(x):
    return x * jax.nn.sigmoid(x)


def _dot(a, b):
    return jnp.dot(a, b, preferred_element_type=F32)


def _ffn_kernel(x_ref, g_ref, wg_ref, wu_ref, wd_ref, gf_ref, o_ref, h_ref, a0_ref, a1_ref, *, nf, final_norm):
    j = pl.program_id(1)

    def up(a_ref):
        h = h_ref[...]
        a_ref[...] = 0.5 * _silu(_dot(h, wg_ref[...])) * _dot(h, wu_ref[...])

    def down(a_ref):
        o_ref[...] += _dot(a_ref[...], wd_ref[...])

    @pl.when(j == 0)
    def _():
        x = x_ref[...]
        h_ref[...] = _rms(x, g_ref[...])
        o_ref[...] = x
        up(a0_ref)

    for parity, (cur, prev) in enumerate(((a0_ref, a1_ref), (a1_ref, a0_ref))):
        @pl.when((j > 0) & (j < nf) & (j % 2 == parity))
        def _():
            up(cur)
            down(prev)

    @pl.when(j == nf)
    def _():
        down(a1_ref if nf % 2 == 0 else a0_ref)
        if final_norm:
            o_ref[...] = _rms(o_ref[...], gf_ref[...])


def _ffn(x, g, wg, wu, wd, gf, *, layer, slot, final_norm):
    t, d = x.shape
    f = wg.shape[-1]
    tm, tf = _token_tile(t), FFN_TILE
    nf = f // tf
    return pl.pallas_call(
        functools.partial(_ffn_kernel, nf=nf, final_norm=final_norm),
        grid=(t // tm, nf + 1),
        in_specs=[
            pl.BlockSpec((tm, d), lambda i, j: (i, 0)),
            pl.BlockSpec((None, None, 1, d), lambda i, j: (layer, slot, 0, 0)),
            pl.BlockSpec((None, None, d, tf), lambda i, j: (layer, slot, 0, jnp.minimum(j, nf - 1))),
            pl.BlockSpec((None, None, d, tf), lambda i, j: (layer, slot, 0, jnp.minimum(j, nf - 1))),
            pl.BlockSpec((None, None, tf, d), lambda i, j: (layer, slot, jnp.maximum(j - 1, 0), 0)),
            pl.BlockSpec((1, d), lambda i, j: (0, 0)),
        ],
        out_specs=pl.BlockSpec((tm, d), lambda i, j: (i, 0)),
        out_shape=jax.ShapeDtypeStruct((t, d), F32),
        scratch_shapes=[pltpu.VMEM((tm, d), F32), pltpu.VMEM((tm, tf), F32), pltpu.VMEM((tm, tf), F32)],
        compiler_params=_params("parallel", "arbitrary"),
        name="ffn_final" if final_norm else "ffn",
    )(x, g.reshape(g.shape[0], g.shape[1], 1, d), wg, wu, wd, gf.reshape(1, d))


def _pw1_kernel(x_ref, g_ref, wa_ref, wb_ref, ba_ref, bb_ref, u_ref, h_ref):
    @pl.when(pl.program_id(1) == 0)
    def _():
        h_ref[...] = _rms(x_ref[...], g_ref[...])

    h = h_ref[...]
    a = _dot(h, wa_ref[...]) + ba_ref[...]
    b = _dot(h, wb_ref[...]) + bb_ref[...]
    u_ref[...] = a * jax.nn.sigmoid(b)


def _pw1_glu(x, g, w, b):
    t, d = x.shape
    dc = w.shape[1] // 2
    tm, tn = _token_tile(t), PROJ_TILE
    nc = dc // tn
    b2 = b.reshape(1, 2 * dc)
    return pl.pallas_call(
        _pw1_kernel,
        grid=(t // tm, nc),
        in_specs=[
            pl.BlockSpec((tm, d), lambda i, j: (i, 0)),
            pl.BlockSpec((1, d), lambda i, j: (0, 0)),
            pl.BlockSpec((d, tn), lambda i, j: (0, j)),
            pl.BlockSpec((d, tn), lambda i, j: (0, j + nc)),
            pl.BlockSpec((1, tn), lambda i, j: (0, j)),
            pl.BlockSpec((1, tn), lambda i, j: (0, j + nc)),
        ],
        out_specs=pl.BlockSpec((tm, tn), lambda i, j: (i, j)),
        out_shape=jax.ShapeDtypeStruct((t, dc), F32),
        scratch_shapes=[pltpu.VMEM((tm, d), F32)],
        compiler_params=_params("parallel", "arbitrary"),
        name="conv_pw1_glu",
    )(x, g.reshape(1, d), w, w, b2, b2)


def _conv_window(win, w_ref, b_ref, rc):
    lead = HALO_PAD - CONV_HALO
    acc = jnp.broadcast_to(b_ref[...], (rc, win.shape[1]))
    for s in range(8):
        part = None
        for a in range((CONV_TAPS + lead + 7) // 8):
            k = 8 * a + s - lead
            if 0 <= k < CONV_TAPS:
                term = w_ref[k:k + 1, :] * win[8 * a:8 * a + rc + 8]
                part = term if part is None else part + term
        acc = acc + part[s:s + rc]
    return acc


def _dwconv_prompt_kernel(cur_ref, w_ref, b_ref, y_ref, ext_ref, *, rc):
    lp, tl = cur_ref.shape
    ext_ref[0:HALO_PAD, :] = jnp.zeros((HALO_PAD, tl), F32)
    ext_ref[HALO_PAD:HALO_PAD + lp, :] = cur_ref[...]
    ext_ref[HALO_PAD + lp:HALO_PAD + lp + 8, :] = jnp.zeros((8, tl), F32)

    def body(c, carry):
        r0 = pl.multiple_of(c * rc, 8)
        y_ref[pl.ds(r0, rc), :] = _conv_window(ext_ref[pl.ds(r0, rc + HALO_PAD + 8), :], w_ref, b_ref, rc)
        return carry

    lax.fori_loop(0, lp // rc, body, 0)


def _dwconv_prompt(u, w, b, *, n_seq, seq_len):
    d = u.shape[1]
    tl = CONV_LANE_TILE
    rc = max(r for r in range(8, 65, 8) if seq_len % r == 0)
    return pl.pallas_call(
        functools.partial(_dwconv_prompt_kernel, rc=rc),
        grid=(n_seq, d // tl),
        in_specs=[
            pl.BlockSpec((seq_len, tl), lambda i, j: (i, j)),
            pl.BlockSpec((CONV_TAPS, tl), lambda i, j: (0, j)),
            pl.BlockSpec((1, tl), lambda i, j: (0, j)),
        ],
        out_specs=pl.BlockSpec((seq_len, tl), lambda i, j: (i, j)),
        out_shape=jax.ShapeDtypeStruct(u.shape, F32),
        scratch_shapes=[pltpu.VMEM((HALO_PAD + seq_len + 8, tl), F32)],
        compiler_params=_params("parallel", "parallel"),
        name="dwconv_prompt",
    )(u, w, b.reshape(1, d))


def _dwconv_sample_kernel(st_ref, u_ref, w_ref, b_ref, y_hbm_ref, y_ref, ext_ref):
    del y_hbm_ref
    t = pl.program_id(2)

    @pl.when(t == 0)
    def _():
        ext_ref[0:CONV_HALO] = st_ref[...]
        ext_ref[CONV_HALO:CONV_HALO + u_ref.shape[0]] = u_ref[...]

    acc = jnp.broadcast_to(b_ref[...], y_ref.shape)
    for k in range(CONV_TAPS):
        acc = acc + w_ref[k:k + 1, :] * ext_ref[t + k]
    y_ref[...] = acc


def _dwconv_sample(y_flat, state_t, u_t, w, b, *, row0):
    ls, bs, d = u_t.shape
    tl = CONV_LANE_TILE
    sb = math.gcd(row0, bs)
    assert sb % 8 == 0
    nsb = bs // sb
    return pl.pallas_call(
        _dwconv_sample_kernel,
        grid=(d // tl, nsb, ls),
        in_specs=[
            pl.BlockSpec((CONV_HALO, sb, tl), lambda j, s, t: (0, s, j)),
            pl.BlockSpec((ls, sb, tl), lambda j, s, t: (0, s, j)),
            pl.BlockSpec((CONV_TAPS, tl), lambda j, s, t: (0, j)),
            pl.BlockSpec((1, tl), lambda j, s, t: (0, j)),
            pl.BlockSpec(memory_space=pl.ANY),
        ],
        out_specs=pl.BlockSpec((sb, tl), lambda j, s, t: (row0 // sb + t * nsb + s, j)),
        out_shape=jax.ShapeDtypeStruct(y_flat.shape, F32),
        scratch_shapes=[pltpu.VMEM((CONV_HALO + ls, sb, tl), F32)],
        input_output_aliases={4: 0},
        compiler_params=_params("parallel", "parallel", "arbitrary"),
        name="dwconv_sample",
    )(state_t, u_t, w, b.reshape(1, d), y_flat)


def _ln_pw2_kernel(y_ref, x_ref, lg_ref, lb_ref, w_ref, b_ref, o_ref, z_ref):
    @pl.when(pl.program_id(1) == 0)
    def _():
        y = y_ref[...]
        mu = jnp.mean(y, axis=-1, keepdims=True)
        yc = y - mu
        var = jnp.mean(yc * yc, axis=-1, keepdims=True)
        z_ref[...] = _silu(yc * lax.rsqrt(var + EPS) * lg_ref[...] + lb_ref[...])

    o_ref[...] = x_ref[...] + _dot(z_ref[...], w_ref[...]) + b_ref[...]


def _ln_pw2(y, x, lg, lb, w, b):
    t, d = x.shape
    dc = y.shape[1]
    tm, tn = _token_tile(t), PROJ_TILE
    return pl.pallas_call(
        _ln_pw2_kernel,
        grid=(t // tm, d // tn),
        in_specs=[
            pl.BlockSpec((tm, dc), lambda i, j: (i, 0)),
            pl.BlockSpec((tm, tn), lambda i, j: (i, j)),
            pl.BlockSpec((1, dc), lambda i, j: (0, 0)),
            pl.BlockSpec((1, dc), lambda i, j: (0, 0)),
            pl.BlockSpec((dc, tn), lambda i, j: (0, j)),
            pl.BlockSpec((1, tn), lambda i, j: (0, j)),
        ],
        out_specs=pl.BlockSpec((tm, tn), lambda i, j: (i, j)),
        out_shape=jax.ShapeDtypeStruct((t, d), F32),
        scratch_shapes=[pltpu.VMEM((tm, dc), F32)],
        compiler_params=_params("parallel", "arbitrary"),
        name="conv_ln_pw2",
    )(y, x, lg.reshape(1, dc), lb.reshape(1, dc), w, b.reshape(1, d))


def _log_sigmoid(z):
    return jnp.minimum(z, 0.0) - jnp.log1p(jnp.exp(-jnp.abs(z)))


def _gla_proj_kernel(x_ref, g_ref, w_ref, wg1_ref, wg2_ref, bg_ref, p_ref, la_ref, h_ref):
    @pl.when(pl.program_id(1) == 0)
    def _():
        h = _rms(x_ref[...], g_ref[...])
        h_ref[...] = h
        z = _dot(_dot(h, wg1_ref[...]), wg2_ref[...]) + bg_ref[...]
        la_ref[...] = _log_sigmoid(z) * (1.0 / GLA_TAU)

    p_ref[...] = _dot(h_ref[...], w_ref[...])


def _gla_proj(x, g, w_cat, wg1, wg2, bg):
    t, d = x.shape
    n = w_cat.shape[1]
    r, nk = wg2.shape
    tm, tn = _token_tile(t), PROJ_TILE
    return pl.pallas_call(
        _gla_proj_kernel,
        grid=(t // tm, n // tn),
        in_specs=[
            pl.BlockSpec((tm, d), lambda i, j: (i, 0)),
            pl.BlockSpec((1, d), lambda i, j: (0, 0)),
            pl.BlockSpec((d, tn), lambda i, j: (0, j)),
            pl.BlockSpec((d, r), lambda i, j: (0, 0)),
            pl.BlockSpec((r, nk), lambda i, j: (0, 0)),
            pl.BlockSpec((1, nk), lambda i, j: (0, 0)),
        ],
        out_specs=[
            pl.BlockSpec((tm, tn), lambda i, j: (i, j)),
            pl.BlockSpec((tm, nk), lambda i, j: (i, 0)),
        ],
        out_shape=[jax.ShapeDtypeStruct((t, n), F32), jax.ShapeDtypeStruct((t, nk), F32)],
        scratch_shapes=[pltpu.VMEM((tm, d), F32)],
        compiler_params=_params("parallel", "arbitrary"),
        name="gla_proj",
    )(x, g.reshape(1, d), w_cat, wg1, wg2, bg.reshape(1, nk))


def _bf16_limbs(x):
    hi = x.astype(jnp.bfloat16).astype(F32)
    mid = (x - hi).astype(jnp.bfloat16).astype(F32)
    return hi, mid, x - hi - mid


def _gla_gate(o, r, gn):
    return o * lax.rsqrt(jnp.mean(o * o, axis=-1, keepdims=True) + EPS) * gn * _silu(r)


def _gla_chunk(q, k, v, la, s0):
    c, dk = q.shape
    n_sub = c // GLA_SUB
    neg_inf = -jnp.inf
    row = lax.broadcasted_iota(jnp.int32, (c, c), 0)
    col = lax.broadcasted_iota(jnp.int32, (c, c), 1)
    tri = jnp.where(row >= col, 1.0, 0.0).astype(F32)
    b = _dot(jnp.concatenate([tri, tri, tri], axis=1), jnp.concatenate(_bf16_limbs(la), axis=0)) * LOG2_E

    o = _dot(q * jnp.exp2(b), s0)

    row_c = lax.broadcasted_iota(jnp.int32, (c, 1), 0)
    lane_c = lax.broadcasted_iota(jnp.int32, (GLA_SUB, c), 1)
    t_sub = lax.broadcasted_iota(jnp.int32, (GLA_SUB, 1), 0)
    blocks = []
    for i in range(n_sub):
        lo = GLA_SUB * i
        qi, ki, bi = q[lo:lo + GLA_SUB], k[lo:lo + GLA_SUB], b[lo:lo + GLA_SUB]
        if i == 0:
            sc = jnp.zeros((GLA_SUB, c), F32)
        else:
            b_start = b[lo - 1:lo]
            qt = qi * jnp.exp2(bi - b_start)
            kt = k * jnp.exp2(jnp.where(row_c < lo, b_start - b, neg_inf))
            sc = lax.dot_general(qt, kt, (((1,), (1,)), ((), ())), preferred_element_type=F32)
        for s in range(GLA_SUB):
            colv = jnp.sum(qi * jnp.exp2(bi - bi[s:s + 1]) * ki[s:s + 1], axis=-1, keepdims=True)
            sc = sc + jnp.where(lane_c == lo + s, jnp.where(t_sub >= s, colv, 0.0), 0.0)
        blocks.append(sc)
    scores = blocks[0] if n_sub == 1 else jnp.concatenate(blocks, axis=0)
    o = o + _dot(scores, v)

    b_last = b[c - 1:c]
    kd = k * jnp.exp2(b_last - b)
    upd = lax.dot_general(kd, v, (((0,), (0,)), ((), ())), preferred_element_type=F32)
    s_new = _column_scale(jnp.exp2(b_last), s0) + upd
    return o, s_new


def _column_scale(d_row, s):
    dk, dv = s.shape
    d_col = jnp.transpose(jnp.broadcast_to(d_row, (128, dk)))
    return jnp.concatenate([d_col] * (dv // 128), axis=1) * s


def _gla_prompt_kernel(q_ref, k_ref, v_ref, r_ref, la_ref, gn_ref, og_ref, sn_ref, s_ref):
    rb = q_ref.shape[0]
    hps, dk, dv = s_ref.shape
    q_scale = dk ** -0.5
    j = pl.program_id(2)

    @pl.when(j == 0)
    def _():
        s_ref[...] = jnp.zeros_like(s_ref)

    def run(r0, c):
        rows = pl.ds(r0, c)
        for h in range(hps):
            kc, vc = slice(h * dk, (h + 1) * dk), slice(h * dv, (h + 1) * dv)
            o, s_new = _gla_chunk(q_ref[rows, kc] * q_scale, k_ref[rows, kc], v_ref[rows, vc], la_ref[rows, kc],
                                  s_ref[h])
            s_ref[h] = s_new
            og_ref[rows, vc] = _gla_gate(o, r_ref[rows, vc], gn_ref[...])

    n_full = rb // GLA_CHUNK

    def body(i, carry):
        run(pl.multiple_of(i * GLA_CHUNK, GLA_CHUNK), GLA_CHUNK)
        return carry

    lax.fori_loop(0, n_full, body, 0)
    if rb > n_full * GLA_CHUNK:
        run(n_full * GLA_CHUNK, rb - n_full * GLA_CHUNK)

    @pl.when(j == pl.num_programs(2) - 1)
    def _():
        sn_ref[0] = s_ref[...]


def _gla_prompt(p, la, gn, *, n_seq, seq_len):
    h, hps = GLA_HEADS, GLA_HEADS_PER_STEP
    dk = la.shape[1] // h
    dv = gn.shape[0]
    rb = max(r for r in range(GLA_SUB, 1025, GLA_SUB) if seq_len % r == 0)
    nrb, ng = seq_len // rb, h // hps
    assert dv == 2 * dk
    return pl.pallas_call(
        _gla_prompt_kernel,
        grid=(n_seq, ng, nrb),
        in_specs=[
            pl.BlockSpec((rb, hps * dk), lambda b, g, j: (b * nrb + j, g)),
            pl.BlockSpec((rb, hps * dk), lambda b, g, j: (b * nrb + j, ng + g)),
            pl.BlockSpec((rb, hps * dv), lambda b, g, j: (b * nrb + j, ng + g)),
            pl.BlockSpec((rb, hps * dv), lambda b, g, j: (b * nrb + j, 2 * ng + g)),
            pl.BlockSpec((rb, hps * dk), lambda b, g, j: (b * nrb + j, g)),
            pl.BlockSpec((1, dv), lambda b, g, j: (0, 0)),
        ],
        out_specs=[
            pl.BlockSpec((rb, hps * dv), lambda b, g, j: (b * nrb + j, g)),
            pl.BlockSpec((1, hps, dk, dv), lambda b, g, j: (b, g, 0, 0)),
        ],
        out_shape=[jax.ShapeDtypeStruct((p.shape[0], h * dv), F32),
                   jax.ShapeDtypeStruct((n_seq, h, dk, dv), F32)],
        scratch_shapes=[pltpu.VMEM((hps, dk, dv), F32)],
        compiler_params=_params("parallel", "parallel", "arbitrary"),
        name="gla_prompt",
    )(p, p, p, p, la, gn.reshape(1, dv))


def _gla_sample_kernel(p_ref, la_ref, s_ref, gn_ref, og_ref, sn_ref, *, seq_len):
    rows = p_ref.shape[0]
    n_seq = rows // seq_len
    dk = la_ref.shape[1] // GLA_HEADS
    dv = 2 * dk
    q_scale = dk ** -0.5
    neg_inf = -jnp.inf
    t_row = lax.broadcasted_iota(jnp.int32, (rows, 1), 0)
    for hd in range(GLA_HEADS):
        q = p_ref[:, hd * dk:(hd + 1) * dk] * q_scale
        k = p_ref[:, (GLA_HEADS + hd) * dk:(GLA_HEADS + hd + 1) * dk]
        v = p_ref[:, (GLA_HEADS + hd) * dv:(GLA_HEADS + hd + 1) * dv]
        r = p_ref[:, (2 * GLA_HEADS + hd) * dv:(2 * GLA_HEADS + hd + 1) * dv]
        la = la_ref[:, hd * dk:(hd + 1) * dk]

        def in_seq_from(s):
            return (t_row >= s) & (t_row < (s // seq_len + 1) * seq_len)

        b = jnp.zeros_like(la)
        for s in range(rows):
            b = b + jnp.where(in_seq_from(s), la[s:s + 1], 0.0)

        qe = q * jnp.exp(b)
        o = jnp.zeros((rows, dv), F32)
        for g in range(n_seq):
            mine = (t_row >= g * seq_len) & (t_row < (g + 1) * seq_len)
            o = jnp.where(mine, _dot(qe, s_ref[g, hd]), o)
        for s in range(rows):
            dec = jnp.exp(jnp.where(in_seq_from(s), b - b[s:s + 1], neg_inf))
            colv = jnp.sum(q * dec * k[s:s + 1], axis=-1, keepdims=True)
            o = o + colv * v[s:s + 1]
        og_ref[:, hd * dv:(hd + 1) * dv] = _gla_gate(o, r, gn_ref[...])

        for g in range(n_seq):
            last = (g + 1) * seq_len - 1
            b_last = b[last:last + 1]
            mine = (t_row >= g * seq_len) & (t_row < (g + 1) * seq_len)
            kd = k * jnp.exp(jnp.where(mine, b_last - b, neg_inf))
            upd = lax.dot_general(kd, v, (((0,), (0,)), ((), ())), preferred_element_type=F32)
            sn_ref[g, hd] = _column_scale(jnp.exp(b_last), s_ref[g, hd]) + upd


def _gla_sample(p_s, la_s, state, gn, *, seq_len):
    rows = 8
    n, h, dk, dv = state.shape
    t = p_s.shape[0]
    per = rows // seq_len
    return pl.pallas_call(
        functools.partial(_gla_sample_kernel, seq_len=seq_len),
        grid=(t // rows,),
        in_specs=[
            pl.BlockSpec((rows, p_s.shape[1]), lambda i: (i, 0)),
            pl.BlockSpec((rows, la_s.shape[1]), lambda i: (i, 0)),
            pl.BlockSpec((per, h, dk, dv), lambda i: (i, 0, 0, 0)),
            pl.BlockSpec((1, dv), lambda i: (0, 0)),
        ],
        out_specs=[
            pl.BlockSpec((rows, h * dv), lambda i: (i, 0)),
            pl.BlockSpec((per, h, dk, dv), lambda i: (i, 0, 0, 0)),
        ],
        out_shape=[jax.ShapeDtypeStruct((t, h * dv), F32), jax.ShapeDtypeStruct(state.shape, F32)],
        compiler_params=_params("parallel"),
        name="gla_sample",
    )(p_s, la_s, state, gn.reshape(1, dv))


def _out_proj_kernel(a_ref, x_ref, w_ref, o_ref):
    o_ref[...] = x_ref[...] + _dot(a_ref[...], w_ref[...])


def _out_proj(a, x, w):
    t, d = x.shape
    kdim = a.shape[1]
    tm, tn = _token_tile(t), PROJ_TILE
    return pl.pallas_call(
        _out_proj_kernel,
        grid=(t // tm, d // tn),
        in_specs=[
            pl.BlockSpec((tm, kdim), lambda i, j: (i, 0)),
            pl.BlockSpec((tm, tn), lambda i, j: (i, j)),
            pl.BlockSpec((kdim, tn), lambda i, j: (0, j)),
        ],
        out_specs=pl.BlockSpec((tm, tn), lambda i, j: (i, j)),
        out_shape=jax.ShapeDtypeStruct((t, d), F32),
        compiler_params=_params("parallel", "arbitrary"),
        name="gla_out_proj",
    )(a, x, w)


def kernel(x_prompt, x_sample, state_conv, state_gla, meta_tokens, norm_ffn, w_ffn_gate, w_ffn_up, w_ffn_down,
           norm_mix, conv_w_pw1, conv_b_pw1, conv_w_dw, conv_b_dw, conv_ln_g, conv_ln_b, conv_w_pw2, conv_b_pw2,
           gla_w_q, gla_w_k, gla_w_v, gla_w_g1, gla_w_g2, gla_b_g, gla_w_r, gla_gn_g, gla_w_o, norm_final):
    bp, sp, d = x_prompt.shape
    bs, ls, _ = x_sample.shape
    lp = N_META_ROWS + sp
    tp, ts = bp * lp, bs * ls
    assert 8 % ls == 0 and (tp + ts) % (8 * _token_blocks(tp + ts)) == 0

    x = jnp.concatenate(
        [piece for b in range(bp) for piece in (meta_tokens, x_prompt[b])]
        + [jnp.transpose(x_sample, (1, 0, 2)).reshape(ts, d)], axis=0)

    def ffn(x, i, j, final_norm=False):
        return _ffn(x, norm_ffn, w_ffn_gate, w_ffn_up, w_ffn_down, norm_final,
                    layer=i, slot=j, final_norm=final_norm)

    def prompt_rows(a, start):
        return jnp.stack([a[b * lp + start:(b + 1) * lp] for b in range(bp)])

    x = ffn(x, 0, 0)
    u = _pw1_glu(x, norm_mix[0], conv_w_pw1[0], conv_b_pw1[0])
    u_s = u[tp:].reshape(ls, bs, d)
    y = _dwconv_prompt(u, conv_w_dw[0], conv_b_dw[0], n_seq=bp, seq_len=lp)
    y = _dwconv_sample(y, jnp.transpose(state_conv[0], (1, 0, 2)), u_s, conv_w_dw[0], conv_b_dw[0], row0=tp)
    x = _ln_pw2(y, x, conv_ln_g[0], conv_ln_b[0], conv_w_pw2[0], conv_b_pw2[0])
    x = ffn(x, 0, 1)
    new_conv_prompt = prompt_rows(u, lp - CONV_HALO)[None]
    new_conv_sample = jnp.concatenate([state_conv[0], jnp.transpose(u_s, (1, 0, 2))], axis=1)[None, :, ls:]

    x = ffn(x, 1, 0)
    w_cat = jnp.concatenate([gla_w_q[0], gla_w_k[0], gla_w_v[0], gla_w_r[0]], axis=1)
    p, la = _gla_proj(x, norm_mix[1], w_cat, gla_w_g1[0], gla_w_g2[0], gla_b_g[0])
    og, new_gla_prompt = _gla_prompt(p, la, gla_gn_g[0], n_seq=bp, seq_len=lp)

    def seq_major(a):
        return jnp.transpose(a[tp:].reshape(ls, bs, -1), (1, 0, 2)).reshape(ts, -1)

    og_s, new_gla_sample = _gla_sample(seq_major(p), seq_major(la), state_gla[0], gla_gn_g[0], seq_len=ls)
    og_s = jnp.transpose(og_s.reshape(bs, ls, -1), (1, 0, 2)).reshape(ts, -1)
    og = lax.dynamic_update_slice(og, og_s, (tp, 0))
    x = _out_proj(og, x, gla_w_o[0])
    x = ffn(x, 1, 1, final_norm=True)

    y_prompt = prompt_rows(x, N_META_ROWS)
    y_sample = jnp.transpose(x[tp:].reshape(ls, bs, d), (1, 0, 2))
    return (y_prompt, y_sample, new_conv_prompt, new_gla_prompt[None], new_conv_sample, new_gla_sample[None])
```

```python
import functools
import math

import jax
import jax.numpy as jnp
from jax import lax
from jax.experimental import pallas as pl
from jax.experimental.pallas import tpu as pltpu

F32 = jnp.float32
EPS = 1e-6
N_META_ROWS = 16
CONV_TAPS = 31
CONV_HALO = CONV_TAPS - 1
HALO_PAD = 32
GLA_HEADS = 4
GLA_HEADS_PER_STEP = 2
GLA_SUB = 16
GLA_CHUNK = 64
GLA_TAU = 16.0
LOG2_E = 1.4426950408889634
V7X_VMEM_LIMIT = 60000 * 1024

TOKEN_ROW_MULTIPLE = 128
MAX_TOKEN_TILE = 1152
FFN_TILE = 256
PROJ_GROUP = 2048
CONV_LANE_TILE = 256
CONV_SAMPLE_LANE_TILE = 512


def _token_tile(t_pad):
    nb = -(-t_pad // MAX_TOKEN_TILE)
    assert t_pad % (16 * nb) == 0, t_pad
    return t_pad // nb


def _proj_tile(t_pad):
    return _token_tile(t_pad) // 2


def _params(*sem):
    return pltpu.CompilerParams(dimension_semantics=sem, vmem_limit_bytes=V7X_VMEM_LIMIT)


def _resident(shape, index_map):
    return pl.BlockSpec(shape, index_map, pipeline_mode=pl.Buffered(1))


def _rms(x, g):
    return x * lax.rsqrt(jnp.mean(x * x, axis=-1, keepdims=True) + EPS) * g


def _silu(x):
    return x * jax.nn.sigmoid(x)


def _dot(a, b):
    return jnp.dot(a, b, preferred_element_type=F32)


def _ffn_kernel(x_ref, g_ref, wg_ref, wu_ref, wd_ref, gf_ref, o_ref, h_ref, *, final_norm):
    f = pl.program_id(1)

    @pl.when(f == 0)
    def _():
        x = x_ref[...]
        h_ref[...] = _rms(x, g_ref[...])
        o_ref[...] = x

    h = h_ref[...]
    a = 0.5 * _silu(_dot(h, wg_ref[...])) * _dot(h, wu_ref[...])
    o_ref[...] += _dot(a, wd_ref[...])

    if final_norm:
        @pl.when(f == pl.num_programs(1) - 1)
        def _():
            o_ref[...] = _rms(o_ref[...], gf_ref[...])


def _ffn(x, g, wg, wu, wd, gf, *, layer, slot, final_norm):
    t, d = x.shape
    f = wg.shape[-1]
    tm, tf = _token_tile(t), FFN_TILE
    return pl.pallas_call(
        functools.partial(_ffn_kernel, final_norm=final_norm),
        grid=(t // tm, f // tf),
        in_specs=[
            pl.BlockSpec((tm, d), lambda i, j: (i, 0)),
            pl.BlockSpec((None, None, 1, d), lambda i, j: (layer, slot, 0, 0)),
            pl.BlockSpec((None, None, d, tf), lambda i, j: (layer, slot, 0, j)),
            pl.BlockSpec((None, None, d, tf), lambda i, j: (layer, slot, 0, j)),
            pl.BlockSpec((None, None, tf, d), lambda i, j: (layer, slot, j, 0)),
            pl.BlockSpec((1, d), lambda i, j: (0, 0)),
        ],
        out_specs=pl.BlockSpec((tm, d), lambda i, j: (i, 0)),
        out_shape=jax.ShapeDtypeStruct((t, d), F32),
        scratch_shapes=[pltpu.VMEM((tm, d), F32)],
        compiler_params=_params("parallel", "arbitrary"),
        name="ffn_final" if final_norm else "ffn",
    )(x, g.reshape(g.shape[0], g.shape[1], 1, d), wg, wu, wd, gf.reshape(1, d))


def _pw1_kernel(x_ref, g_ref, wa_ref, wb_ref, ba_ref, bb_ref, u_ref):
    h = _rms(x_ref[...], g_ref[...])
    a = _dot(h, wa_ref[...]) + ba_ref[...]
    b = _dot(h, wb_ref[...]) + bb_ref[...]
    u_ref[...] = a * jax.nn.sigmoid(b)


def _pw1_glu(x, g, w, b):
    t, d = x.shape
    dc = w.shape[1] // 2
    tm, tn = _proj_tile(t), PROJ_GROUP // 2
    nc = dc // tn
    b2 = b.reshape(1, 2 * dc)
    return pl.pallas_call(
        _pw1_kernel,
        grid=(nc, t // tm),
        in_specs=[
            pl.BlockSpec((tm, d), lambda c, i: (i, 0)),
            pl.BlockSpec((1, d), lambda c, i: (0, 0)),
            _resident((d, tn), lambda c, i: (0, c)),
            _resident((d, tn), lambda c, i: (0, c + nc)),
            pl.BlockSpec((1, tn), lambda c, i: (0, c)),
            pl.BlockSpec((1, tn), lambda c, i: (0, c + nc)),
        ],
        out_specs=pl.BlockSpec((tm, tn), lambda c, i: (i, c)),
        out_shape=jax.ShapeDtypeStruct((t, dc), F32),
        compiler_params=_params("arbitrary", "arbitrary"),
        name="conv_pw1_glu",
    )(x, g.reshape(1, d), w, w, b2, b2)


def _conv_window(win, w_ref, b_ref, rc):
    lead = HALO_PAD - CONV_HALO
    acc = jnp.broadcast_to(b_ref[...], (rc, win.shape[1]))
    for s in range(8):
        part = None
        for a in range((CONV_TAPS + lead + 7) // 8):
            k = 8 * a + s - lead
            if 0 <= k < CONV_TAPS:
                term = w_ref[k:k + 1, :] * win[8 * a:8 * a + rc + 8]
                part = term if part is None else part + term
        acc = acc + part[s:s + rc]
    return acc


def _dwconv_prompt_kernel(cur_ref, w_ref, b_ref, y_ref, ext_ref, *, rc):
    lp, tl = cur_ref.shape
    ext_ref[0:HALO_PAD, :] = jnp.zeros((HALO_PAD, tl), F32)
    ext_ref[HALO_PAD:HALO_PAD + lp, :] = cur_ref[...]
    ext_ref[HALO_PAD + lp:HALO_PAD + lp + 8, :] = jnp.zeros((8, tl), F32)

    def body(c, carry):
        r0 = pl.multiple_of(c * rc, 8)
        y_ref[pl.ds(r0, rc), :] = _conv_window(ext_ref[pl.ds(r0, rc + HALO_PAD + 8), :], w_ref, b_ref, rc)
        return carry

    lax.fori_loop(0, lp // rc, body, 0)


def _dwconv_prompt(u, w, b, *, n_seq, seq_len):
    d = u.shape[1]
    tl = CONV_LANE_TILE
    rc = max(r for r in range(8, 65, 8) if seq_len % r == 0)
    return pl.pallas_call(
        functools.partial(_dwconv_prompt_kernel, rc=rc),
        grid=(n_seq, d // tl),
        in_specs=[
            pl.BlockSpec((seq_len, tl), lambda i, j: (i, j)),
            pl.BlockSpec((CONV_TAPS, tl), lambda i, j: (0, j)),
            pl.BlockSpec((1, tl), lambda i, j: (0, j)),
        ],
        out_specs=pl.BlockSpec((seq_len, tl), lambda i, j: (i, j)),
        out_shape=jax.ShapeDtypeStruct(u.shape, F32),
        scratch_shapes=[pltpu.VMEM((HALO_PAD + seq_len + 8, tl), F32)],
        compiler_params=_params("parallel", "parallel"),
        name="dwconv_prompt",
    )(u, w, b.reshape(1, d))


def _dwconv_sample_kernel(st_ref, u_ref, w_ref, b_ref, y_hbm_ref, y_ref, ext_ref):
    del y_hbm_ref
    ls = u_ref.shape[0]
    t = pl.program_id(2)

    @pl.when(t == 0)
    def _():
        ext_ref[0:CONV_HALO] = st_ref[...]
        ext_ref[CONV_HALO:CONV_HALO + ls] = u_ref[...]

    @pl.when(t < ls)
    def _():
        acc = jnp.broadcast_to(b_ref[...], y_ref.shape)
        for k in range(CONV_TAPS):
            acc = acc + w_ref[k:k + 1, :] * ext_ref[t + k]
        y_ref[...] = acc

    @pl.when(t >= ls)
    def _():
        y_ref[...] = jnp.zeros_like(y_ref)


def _dwconv_sample(y_flat, state_t, u_t, w, b, *, row0):
    ls, bs, d = u_t.shape
    tl = CONV_SAMPLE_LANE_TILE
    sb = math.gcd(row0, bs)
    nsb = bs // sb
    n_pad = y_flat.shape[0] - row0 - ls * bs
    assert sb % 8 == 0 and n_pad % sb == 0
    last_block = ls * nsb + n_pad // sb - 1
    return pl.pallas_call(
        _dwconv_sample_kernel,
        grid=(d // tl, nsb, ls + -(-n_pad // (sb * nsb))),
        in_specs=[
            pl.BlockSpec((CONV_HALO, sb, tl), lambda j, s, t: (0, s, j)),
            pl.BlockSpec((ls, sb, tl), lambda j, s, t: (0, s, j)),
            pl.BlockSpec((CONV_TAPS, tl), lambda j, s, t: (0, j)),
            pl.BlockSpec((1, tl), lambda j, s, t: (0, j)),
            pl.BlockSpec(memory_space=pl.ANY),
        ],
        out_specs=pl.BlockSpec((sb, tl), lambda j, s, t: (row0 // sb + jnp.minimum(t * nsb + s, last_block), j)),
        out_shape=jax.ShapeDtypeStruct(y_flat.shape, F32),
        scratch_shapes=[pltpu.VMEM((CONV_HALO + ls, sb, tl), F32)],
        input_output_aliases={4: 0},
        compiler_params=_params("parallel", "arbitrary", "arbitrary"),
        name="dwconv_sample",
    )(state_t, u_t, w, b.reshape(1, d), y_flat)


def _ln_pw2_kernel(y_ref, x_ref, lg_ref, lb_ref, w_ref, b_ref, o_ref):
    y = y_ref[...]
    mu = jnp.mean(y, axis=-1, keepdims=True)
    yc = y - mu
    var = jnp.mean(yc * yc, axis=-1, keepdims=True)
    z = _silu(yc * lax.rsqrt(var + EPS) * lg_ref[...] + lb_ref[...])
    o_ref[...] = x_ref[...] + _dot(z, w_ref[...]) + b_ref[...]


def _ln_pw2(y, x, lg, lb, w, b):
    t, d = x.shape
    dc = y.shape[1]
    tm = _proj_tile(t)
    return pl.pallas_call(
        _ln_pw2_kernel,
        grid=(t // tm,),
        in_specs=[
            pl.BlockSpec((tm, dc), lambda i: (i, 0)),
            pl.BlockSpec((tm, d), lambda i: (i, 0)),
            pl.BlockSpec((1, dc), lambda i: (0, 0)),
            pl.BlockSpec((1, dc), lambda i: (0, 0)),
            _resident((dc, d), lambda i: (0, 0)),
            pl.BlockSpec((1, d), lambda i: (0, 0)),
        ],
        out_specs=pl.BlockSpec((tm, d), lambda i: (i, 0)),
        out_shape=jax.ShapeDtypeStruct((t, d), F32),
        compiler_params=_params("parallel"),
        name="conv_ln_pw2",
    )(y, x, lg.reshape(1, dc), lb.reshape(1, dc), w, b.reshape(1, d))


def _log_sigmoid(z):
    return jnp.minimum(z, 0.0) - jnp.log1p(jnp.exp(-jnp.abs(z)))


def _gla_proj_kernel(x_ref, g_ref, w_ref, p_ref):
    p_ref[...] = _dot(_rms(x_ref[...], g_ref[...]), w_ref[...])


def _gla_proj(x, g, w_cat):
    t, d = x.shape
    n = w_cat.shape[1]
    tm, tn = _proj_tile(t), PROJ_GROUP
    return pl.pallas_call(
        _gla_proj_kernel,
        grid=(n // tn, t // tm),
        in_specs=[
            pl.BlockSpec((tm, d), lambda c, i: (i, 0)),
            pl.BlockSpec((1, d), lambda c, i: (0, 0)),
            _resident((d, tn), lambda c, i: (0, c)),
        ],
        out_specs=pl.BlockSpec((tm, tn), lambda c, i: (i, c)),
        out_shape=jax.ShapeDtypeStruct((t, n), F32),
        compiler_params=_params("arbitrary", "arbitrary"),
        name="gla_proj",
    )(x, g.reshape(1, d), w_cat)


def _gla_decay_kernel(x_ref, g_ref, wg1_ref, wg2_ref, bg_ref, la_ref):
    z = _dot(_dot(_rms(x_ref[...], g_ref[...]), wg1_ref[...]), wg2_ref[...]) + bg_ref[...]
    la_ref[...] = _log_sigmoid(z) * (1.0 / GLA_TAU)


def _gla_decay(x, g, wg1, wg2, bg):
    t, d = x.shape
    r, nk = wg2.shape
    tm = _proj_tile(t)
    return pl.pallas_call(
        _gla_decay_kernel,
        grid=(t // tm,),
        in_specs=[
            pl.BlockSpec((tm, d), lambda i: (i, 0)),
            pl.BlockSpec((1, d), lambda i: (0, 0)),
            pl.BlockSpec((d, r), lambda i: (0, 0)),
            pl.BlockSpec((r, nk), lambda i: (0, 0)),
            pl.BlockSpec((1, nk), lambda i: (0, 0)),
        ],
        out_specs=pl.BlockSpec((tm, nk), lambda i: (i, 0)),
        out_shape=jax.ShapeDtypeStruct((t, nk), F32),
        compiler_params=_params("parallel"),
        name="gla_decay",
    )(x, g.reshape(1, d), wg1, wg2, bg.reshape(1, nk))


def _bf16_limbs(x):
    hi = x.astype(jnp.bfloat16).astype(F32)
    mid = (x - hi).astype(jnp.bfloat16).astype(F32)
    return hi, mid, x - hi - mid


def _gla_gate(o, r, gn):
    return o * lax.rsqrt(jnp.mean(o * o, axis=-1, keepdims=True) + EPS) * gn * _silu(r)


def _gla_chunk(q, k, v, la, s0):
    c, dk = q.shape
    n_sub = c // GLA_SUB
    neg_inf = -jnp.inf
    row = lax.broadcasted_iota(jnp.int32, (c, c), 0)
    col = lax.broadcasted_iota(jnp.int32, (c, c), 1)
    tri = jnp.where(row >= col, 1.0, 0.0).astype(F32)
    b = _dot(jnp.concatenate([tri, tri, tri], axis=1), jnp.concatenate(_bf16_limbs(la), axis=0)) * LOG2_E

    o = _dot(q * jnp.exp2(b), s0)

    row_c = lax.broadcasted_iota(jnp.int32, (c, 1), 0)
    lane_c = lax.broadcasted_iota(jnp.int32, (GLA_SUB, c), 1)
    t_sub = lax.broadcasted_iota(jnp.int32, (GLA_SUB, 1), 0)
    blocks = []
    for i in range(n_sub):
        lo = GLA_SUB * i
        qi, ki, bi = q[lo:lo + GLA_SUB], k[lo:lo + GLA_SUB], b[lo:lo + GLA_SUB]
        if i == 0:
            sc = jnp.zeros((GLA_SUB, c), F32)
        else:
            b_start = b[lo - 1:lo]
            qt = qi * jnp.exp2(bi - b_start)
            kt = k * jnp.exp2(jnp.where(row_c < lo, b_start - b, neg_inf))
            sc = lax.dot_general(qt, kt, (((1,), (1,)), ((), ())), preferred_element_type=F32)
        for s in range(GLA_SUB):
            colv = jnp.sum(qi * jnp.exp2(bi - bi[s:s + 1]) * ki[s:s + 1], axis=-1, keepdims=True)
            sc = sc + jnp.where(lane_c == lo + s, jnp.where(t_sub >= s, colv, 0.0), 0.0)
        blocks.append(sc)
    scores = blocks[0] if n_sub == 1 else jnp.concatenate(blocks, axis=0)
    o = o + _dot(scores, v)

    b_last = b[c - 1:c]
    kd = k * jnp.exp2(b_last - b)
    upd = lax.dot_general(kd, v, (((0,), (0,)), ((), ())), preferred_element_type=F32)
    s_new = _column_scale(jnp.exp2(b_last), s0) + upd
    return o, s_new


def _column_scale(d_row, s):
    dk, dv = s.shape
    d_col = jnp.transpose(jnp.broadcast_to(d_row, (128, dk)))
    return jnp.concatenate([d_col] * (dv // 128), axis=1) * s


def _gla_prompt_kernel(q_ref, k_ref, v_ref, r_ref, la_ref, gn_ref, og_ref, sn_ref, s_ref):
    rb = q_ref.shape[0]
    hps, dk, dv = s_ref.shape
    q_scale = dk ** -0.5
    j = pl.program_id(2)

    @pl.when(j == 0)
    def _():
        s_ref[...] = jnp.zeros_like(s_ref)

    def run(r0, c):
        rows = pl.ds(r0, c)
        for h in range(hps):
            kc, vc = slice(h * dk, (h + 1) * dk), slice(h * dv, (h + 1) * dv)
            o, s_new = _gla_chunk(q_ref[rows, kc] * q_scale, k_ref[rows, kc], v_ref[rows, vc], la_ref[rows, kc],
                                  s_ref[h])
            s_ref[h] = s_new
            og_ref[rows, vc] = _gla_gate(o, r_ref[rows, vc], gn_ref[...])

    n_full = rb // GLA_CHUNK

    def body(i, carry):
        run(pl.multiple_of(i * GLA_CHUNK, GLA_CHUNK), GLA_CHUNK)
        return carry

    lax.fori_loop(0, n_full, body, 0)
    if rb > n_full * GLA_CHUNK:
        run(n_full * GLA_CHUNK, rb - n_full * GLA_CHUNK)

    @pl.when(j == pl.num_programs(2) - 1)
    def _():
        sn_ref[0] = s_ref[...]


def _gla_prompt(p, la, gn, *, n_seq, seq_len):
    h, hps = GLA_HEADS, GLA_HEADS_PER_STEP
    dk = la.shape[1] // h
    dv = gn.shape[0]
    rb = max(r for r in range(GLA_SUB, 1025, GLA_SUB) if seq_len % r == 0)
    nrb, ng = seq_len // rb, h // hps
    assert dv == 2 * dk
    return pl.pallas_call(
        _gla_prompt_kernel,
        grid=(n_seq, ng, nrb),
        in_specs=[
            pl.BlockSpec((rb, hps * dk), lambda b, g, j: (b * nrb + j, g)),
            pl.BlockSpec((rb, hps * dk), lambda b, g, j: (b * nrb + j, ng + g)),
            pl.BlockSpec((rb, hps * dv), lambda b, g, j: (b * nrb + j, ng + g)),
            pl.BlockSpec((rb, hps * dv), lambda b, g, j: (b * nrb + j, 2 * ng + g)),
            pl.BlockSpec((rb, hps * dk), lambda b, g, j: (b * nrb + j, g)),
            pl.BlockSpec((1, dv), lambda b, g, j: (0, 0)),
        ],
        out_specs=[
            pl.BlockSpec((rb, hps * dv), lambda b, g, j: (b * nrb + j, g)),
            pl.BlockSpec((1, hps, dk, dv), lambda b, g, j: (b, g, 0, 0)),
        ],
        out_shape=[jax.ShapeDtypeStruct((p.shape[0], h * dv), F32),
                   jax.ShapeDtypeStruct((n_seq, h, dk, dv), F32)],
        scratch_shapes=[pltpu.VMEM((hps, dk, dv), F32)],
        compiler_params=_params("parallel", "parallel", "arbitrary"),
        name="gla_prompt",
    )(p, p, p, p, la, gn.reshape(1, dv))


def _gla_sample_kernel(p_ref, la_ref, s_ref, gn_ref, og_ref, sn_ref, *, seq_len):
    rows = p_ref.shape[0]
    n_seq = rows // seq_len
    dk = la_ref.shape[1] // GLA_HEADS
    dv = 2 * dk
    q_scale = dk ** -0.5
    neg_inf = -jnp.inf
    t_row = lax.broadcasted_iota(jnp.int32, (rows, 1), 0)
    for hd in range(GLA_HEADS):
        q = p_ref[:, hd * dk:(hd + 1) * dk] * q_scale
        k = p_ref[:, (GLA_HEADS + hd) * dk:(GLA_HEADS + hd + 1) * dk]
        v = p_ref[:, (GLA_HEADS + hd) * dv:(GLA_HEADS + hd + 1) * dv]
        r = p_ref[:, (2 * GLA_HEADS + hd) * dv:(2 * GLA_HEADS + hd + 1) * dv]
        la = la_ref[:, hd * dk:(hd + 1) * dk]

        def in_seq_from(s):
            return (t_row >= s) & (t_row < (s // seq_len + 1) * seq_len)

        b = jnp.zeros_like(la)
        for s in range(rows):
            b = b + jnp.where(in_seq_from(s), la[s:s + 1], 0.0)

        qe = q * jnp.exp(b)
        o = jnp.zeros((rows, dv), F32)
        for g in range(n_seq):
            mine = (t_row >= g * seq_len) & (t_row < (g + 1) * seq_len)
            o = jnp.where(mine, _dot(qe, s_ref[g, hd]), o)
        for s in range(rows):
            dec = jnp.exp(jnp.where(in_seq_from(s), b - b[s:s + 1], neg_inf))
            colv = jnp.sum(q * dec * k[s:s + 1], axis=-1, keepdims=True)
            o = o + colv * v[s:s + 1]
        og_ref[:, hd * dv:(hd + 1) * dv] = _gla_gate(o, r, gn_ref[...])

        for g in range(n_seq):
            last = (g + 1) * seq_len - 1
            b_last = b[last:last + 1]
            mine = (t_row >= g * seq_len) & (t_row < (g + 1) * seq_len)
            kd = k * jnp.exp(jnp.where(mine, b_last - b, neg_inf))
            upd = lax.dot_general(kd, v, (((0,), (0,)), ((), ())), preferred_element_type=F32)
            sn_ref[g, hd] = _column_scale(jnp.exp(b_last), s_ref[g, hd]) + upd


def _gla_sample(p_s, la_s, state, gn, *, seq_len):
    rows = 8
    n, h, dk, dv = state.shape
    t = p_s.shape[0]
    per = rows // seq_len
    return pl.pallas_call(
        functools.partial(_gla_sample_kernel, seq_len=seq_len),
        grid=(t // rows,),
        in_specs=[
            pl.BlockSpec((rows, p_s.shape[1]), lambda i: (i, 0)),
            pl.BlockSpec((rows, la_s.shape[1]), lambda i: (i, 0)),
            pl.BlockSpec((per, h, dk, dv), lambda i: (i, 0, 0, 0)),
            pl.BlockSpec((1, dv), lambda i: (0, 0)),
        ],
        out_specs=[
            pl.BlockSpec((rows, h * dv), lambda i: (i, 0)),
            pl.BlockSpec((per, h, dk, dv), lambda i: (i, 0, 0, 0)),
        ],
        out_shape=[jax.ShapeDtypeStruct((t, h * dv), F32), jax.ShapeDtypeStruct(state.shape, F32)],
        compiler_params=_params("parallel"),
        name="gla_sample",
    )(p_s, la_s, state, gn.reshape(1, dv))


def _out_proj_kernel(a_ref, x_ref, w_ref, o_ref):
    o_ref[...] = x_ref[...] + _dot(a_ref[...], w_ref[...])


def _out_proj(a, x, w):
    t, d = x.shape
    kdim = a.shape[1]
    tm = _proj_tile(t)
    return pl.pallas_call(
        _out_proj_kernel,
        grid=(t // tm,),
        in_specs=[
            pl.BlockSpec((tm, kdim), lambda i: (i, 0)),
            pl.BlockSpec((tm, d), lambda i: (i, 0)),
            _resident((kdim, d), lambda i: (0, 0)),
        ],
        out_specs=pl.BlockSpec((tm, d), lambda i: (i, 0)),
        out_shape=jax.ShapeDtypeStruct((t, d), F32),
        compiler_params=_params("parallel"),
        name="gla_out_proj",
    )(a, x, w)


def kernel(x_prompt, x_sample, state_conv, state_gla, meta_tokens, norm_ffn, w_ffn_gate, w_ffn_up, w_ffn_down,
           norm_mix, conv_w_pw1, conv_b_pw1, conv_w_dw, conv_b_dw, conv_ln_g, conv_ln_b, conv_w_pw2, conv_b_pw2,
           gla_w_q, gla_w_k, gla_w_v, gla_w_g1, gla_w_g2, gla_b_g, gla_w_r, gla_gn_g, gla_w_o, norm_final):
    bp, sp, d = x_prompt.shape
    bs, ls, _ = x_sample.shape
    lp = N_META_ROWS + sp
    tp, ts = bp * lp, bs * ls
    n_pad = -(tp + ts) % TOKEN_ROW_MULTIPLE
    assert 8 % ls == 0 and ts % 8 == 0

    def with_pad(a):
        return jnp.concatenate([a, jnp.zeros((n_pad, a.shape[1]), F32)], axis=0) if n_pad else a

    x = jnp.concatenate(
        [piece for b in range(bp) for piece in (meta_tokens, x_prompt[b])]
        + [with_pad(jnp.transpose(x_sample, (1, 0, 2)).reshape(ts, d))], axis=0)

    def ffn(x, i, j, final_norm=False):
        return _ffn(x, norm_ffn, w_ffn_gate, w_ffn_up, w_ffn_down, norm_final,
                    layer=i, slot=j, final_norm=final_norm)

    def prompt_rows(a, start):
        return jnp.stack([a[b * lp + start:(b + 1) * lp] for b in range(bp)])

    def sample_rows(a):
        return a[tp:tp + ts].reshape(ls, bs, -1)

    x = ffn(x, 0, 0)
    u = _pw1_glu(x, norm_mix[0], conv_w_pw1[0], conv_b_pw1[0])
    u_s = sample_rows(u)
    y = _dwconv_prompt(u, conv_w_dw[0], conv_b_dw[0], n_seq=bp, seq_len=lp)
    y = _dwconv_sample(y, jnp.transpose(state_conv[0], (1, 0, 2)), u_s, conv_w_dw[0], conv_b_dw[0], row0=tp)
    x = _ln_pw2(y, x, conv_ln_g[0], conv_ln_b[0], conv_w_pw2[0], conv_b_pw2[0])
    x = ffn(x, 0, 1)
    new_conv_prompt = prompt_rows(u, lp - CONV_HALO)[None]
    new_conv_sample = jnp.concatenate([state_conv[0], jnp.transpose(u_s, (1, 0, 2))], axis=1)[None, :, ls:]

    x = ffn(x, 1, 0)
    w_cat = jnp.concatenate([gla_w_q[0], gla_w_k[0], gla_w_v[0], gla_w_r[0]], axis=1)
    p = _gla_proj(x, norm_mix[1], w_cat)
    la = _gla_decay(x, norm_mix[1], gla_w_g1[0], gla_w_g2[0], gla_b_g[0])
    og, new_gla_prompt = _gla_prompt(p, la, gla_gn_g[0], n_seq=bp, seq_len=lp)

    def seq_major(a):
        return jnp.transpose(sample_rows(a), (1, 0, 2)).reshape(ts, -1)

    og_s, new_gla_sample = _gla_sample(seq_major(p), seq_major(la), state_gla[0], gla_gn_g[0], seq_len=ls)
    og_s = jnp.transpose(og_s.reshape(bs, ls, -1), (1, 0, 2)).reshape(ts, -1)
    og = lax.dynamic_update_slice(og, with_pad(og_s), (tp, 0))
    x = _out_proj(og, x, gla_w_o[0])
    x = ffn(x, 1, 1, final_norm=True)

    y_prompt = prompt_rows(x, N_META_ROWS)
    y_sample = jnp.transpose(sample_rows(x), (1, 0, 2))
    return (y_prompt, y_sample, new_conv_prompt, new_gla_prompt[None], new_conv_sample, new_gla_sample[None])
```

```python
import functools
import math

import jax
import jax.numpy as jnp
from jax import lax
from jax.experimental import pallas as pl
from jax.experimental.pallas import tpu as pltpu

F32 = jnp.float32
EPS = 1e-6
N_META_ROWS = 16
CONV_TAPS = 31
CONV_HALO = CONV_TAPS - 1
HALO_PAD = 32
GLA_HEADS = 4
GLA_HEADS_PER_STEP = 4
GLA_SUB = 16
GLA_CHUNK = 64
GLA_TAU = 16.0
LOG2_E = 1.4426950408889634
V7X_VMEM_LIMIT = 60000 * 1024

TOKEN_ROW_MULTIPLE = 128
MAX_TOKEN_TILE = 1152
FFN_TILE = 256
PROJ_GROUP = 2048
CONV_LANE_TILE = 256
CONV_SAMPLE_LANE_TILE = 512


def _token_tile(t_pad):
    nb = -(-t_pad // MAX_TOKEN_TILE)
    assert t_pad % (16 * nb) == 0, t_pad
    return t_pad // nb


def _proj_tile(t_pad):
    return _token_tile(t_pad) // 2


def _params(*sem):
    return pltpu.CompilerParams(dimension_semantics=sem, vmem_limit_bytes=V7X_VMEM_LIMIT)


def _resident(shape, index_map):
    return pl.BlockSpec(shape, index_map, pipeline_mode=pl.Buffered(1))


def _rms(x, g):
    return x * lax.rsqrt(jnp.mean(x * x, axis=-1, keepdims=True) + EPS) * g


def _silu(x):
    return x * jax.nn.sigmoid(x)


def _dot(a, b):
    return jnp.dot(a, b, preferred_element_type=F32)


def _ffn_kernel(x_ref, g_ref, wg_ref, wu_ref, wd_ref, gf_ref, o_ref, h_ref, *, final_norm):
    f = pl.program_id(1)

    @pl.when(f == 0)
    def _():
        x = x_ref[...]
        h_ref[...] = _rms(x, g_ref[...])
        o_ref[...] = x

    h = h_ref[...]
    a = 0.5 * _silu(_dot(h, wg_ref[...])) * _dot(h, wu_ref[...])
    o_ref[...] += _dot(a, wd_ref[...])

    if final_norm:
        @pl.when(f == pl.num_programs(1) - 1)
        def _():
            o_ref[...] = _rms(o_ref[...], gf_ref[...])


def _ffn(x, g, wg, wu, wd, gf, *, layer, slot, final_norm):
    t, d = x.shape
    f = wg.shape[-1]
    tm, tf = _token_tile(t), FFN_TILE
    return pl.pallas_call(
        functools.partial(_ffn_kernel, final_norm=final_norm),
        grid=(t // tm, f // tf),
        in_specs=[
            pl.BlockSpec((tm, d), lambda i, j: (i, 0)),
            pl.BlockSpec((None, None, 1, d), lambda i, j: (layer, slot, 0, 0)),
            pl.BlockSpec((None, None, d, tf), lambda i, j: (layer, slot, 0, j)),
            pl.BlockSpec((None, None, d, tf), lambda i, j: (layer, slot, 0, j)),
            pl.BlockSpec((None, None, tf, d), lambda i, j: (layer, slot, j, 0)),
            pl.BlockSpec((1, d), lambda i, j: (0, 0)),
        ],
        out_specs=pl.BlockSpec((tm, d), lambda i, j: (i, 0)),
        out_shape=jax.ShapeDtypeStruct((t, d), F32),
        scratch_shapes=[pltpu.VMEM((tm, d), F32)],
        compiler_params=_params("parallel", "arbitrary"),
        name="ffn_final" if final_norm else "ffn",
    )(x, g.reshape(g.shape[0], g.shape[1], 1, d), wg, wu, wd, gf.reshape(1, d))


def _row_scale(x):
    return lax.rsqrt(jnp.mean(x * x, axis=-1, keepdims=True) + EPS)


def _pw1_kernel(x_ref, g_ref, wa_ref, wb_ref, ba_ref, bb_ref, u_ref):
    x = x_ref[...]
    xg = x * g_ref[...]
    rs = _row_scale(x)
    a = _dot(xg, wa_ref[...]) * rs + ba_ref[...]
    b = _dot(xg, wb_ref[...]) * rs + bb_ref[...]
    u_ref[...] = a * jax.nn.sigmoid(b)


def _pw1_glu(x, g, w, b):
    t, d = x.shape
    dc = w.shape[1] // 2
    tm, tn = _proj_tile(t), PROJ_GROUP // 2
    nc = dc // tn
    b2 = b.reshape(1, 2 * dc)
    return pl.pallas_call(
        _pw1_kernel,
        grid=(nc, t // tm),
        in_specs=[
            pl.BlockSpec((tm, d), lambda c, i: (i, 0)),
            pl.BlockSpec((1, d), lambda c, i: (0, 0)),
            _resident((d, tn), lambda c, i: (0, c)),
            _resident((d, tn), lambda c, i: (0, c + nc)),
            pl.BlockSpec((1, tn), lambda c, i: (0, c)),
            pl.BlockSpec((1, tn), lambda c, i: (0, c + nc)),
        ],
        out_specs=pl.BlockSpec((tm, tn), lambda c, i: (i, c)),
        out_shape=jax.ShapeDtypeStruct((t, dc), F32),
        compiler_params=_params("arbitrary", "arbitrary"),
        name="conv_pw1_glu",
    )(x, g.reshape(1, d), w, w, b2, b2)


def _conv_window(win, wb_ref, b_ref, rc):
    lead = HALO_PAD - CONV_HALO
    tl = win.shape[1]
    n_tiles = rc // 8 + 1
    acc = jnp.broadcast_to(b_ref[...], (rc, tl))
    for s in range(8):
        part = None
        for a in range((CONV_TAPS + lead + 7) // 8):
            k = 8 * a + s - lead
            if 0 <= k < CONV_TAPS:
                term = (win[8 * a:8 * a + rc + 8].reshape(n_tiles, 8, tl) * wb_ref[k][None]).reshape(rc + 8, tl)
                part = term if part is None else part + term
        acc = acc + (part[:rc] if s == 0 else pltpu.roll(part, rc + 8 - s, axis=0)[:rc])
    return acc


def _dwconv_prompt_kernel(cur_ref, w_ref, b_ref, y_ref, ext_ref, wb_ref, *, rc):
    lp, tl = cur_ref.shape
    ext_ref[0:HALO_PAD, :] = jnp.zeros((HALO_PAD, tl), F32)
    ext_ref[HALO_PAD:HALO_PAD + lp, :] = cur_ref[...]
    ext_ref[HALO_PAD + lp:HALO_PAD + lp + 8, :] = jnp.zeros((8, tl), F32)
    for k in range(CONV_TAPS):
        wb_ref[k] = jnp.broadcast_to(w_ref[k:k + 1, :], (8, tl))

    def body(c, carry):
        r0 = pl.multiple_of(c * rc, 8)
        y_ref[pl.ds(r0, rc), :] = _conv_window(ext_ref[pl.ds(r0, rc + HALO_PAD + 8), :], wb_ref, b_ref, rc)
        return carry

    lax.fori_loop(0, lp // rc, body, 0)


def _dwconv_prompt(u, w, b, *, n_seq, seq_len):
    d = u.shape[1]
    tl = CONV_LANE_TILE
    rc = max(r for r in range(8, 65, 8) if seq_len % r == 0)
    return pl.pallas_call(
        functools.partial(_dwconv_prompt_kernel, rc=rc),
        grid=(n_seq, d // tl),
        in_specs=[
            pl.BlockSpec((seq_len, tl), lambda i, j: (i, j)),
            pl.BlockSpec((CONV_TAPS, tl), lambda i, j: (0, j)),
            pl.BlockSpec((1, tl), lambda i, j: (0, j)),
        ],
        out_specs=pl.BlockSpec((seq_len, tl), lambda i, j: (i, j)),
        out_shape=jax.ShapeDtypeStruct(u.shape, F32),
        scratch_shapes=[pltpu.VMEM((HALO_PAD + seq_len + 8, tl), F32), pltpu.VMEM((CONV_TAPS, 8, tl), F32)],
        compiler_params=_params("parallel", "parallel"),
        name="dwconv_prompt",
    )(u, w, b.reshape(1, d))


def _dwconv_sample_kernel(st_ref, u_ref, w_ref, b_ref, y_hbm_ref, y_ref, ext_ref, *, n_conv_steps):
    del y_hbm_ref
    ls = u_ref.shape[0]
    q = pl.program_id(1)
    t = q % ls

    @pl.when((t == 0) & (q < n_conv_steps))
    def _():
        ext_ref[0:CONV_HALO] = st_ref[...]
        ext_ref[CONV_HALO:CONV_HALO + ls] = u_ref[...]

    @pl.when(q < n_conv_steps)
    def _():
        acc = jnp.broadcast_to(b_ref[...], y_ref.shape)
        for k in range(CONV_TAPS):
            acc = acc + w_ref[k:k + 1, :] * ext_ref[t + k]
        y_ref[...] = acc

    @pl.when(q >= n_conv_steps)
    def _():
        y_ref[...] = jnp.zeros_like(y_ref)


def _dwconv_sample(y_flat, state_t, u_t, w, b, *, row0):
    ls, bs, d = u_t.shape
    tl = CONV_SAMPLE_LANE_TILE
    sb = math.gcd(row0, bs)
    nsb = bs // sb
    n_pad = y_flat.shape[0] - row0 - ls * bs
    assert sb % 8 == 0 and n_pad % sb == 0
    n_conv = ls * nsb

    def seq_block(q):
        return jnp.minimum(q // ls, nsb - 1)

    def out_block(q):
        return row0 // sb + jnp.where(q < n_conv, (q % ls) * nsb + q // ls, q)

    return pl.pallas_call(
        functools.partial(_dwconv_sample_kernel, n_conv_steps=n_conv),
        grid=(d // tl, n_conv + n_pad // sb),
        in_specs=[
            pl.BlockSpec((CONV_HALO, sb, tl), lambda j, q: (0, seq_block(q), j)),
            pl.BlockSpec((ls, sb, tl), lambda j, q: (0, seq_block(q), j)),
            pl.BlockSpec((CONV_TAPS, tl), lambda j, q: (0, j)),
            pl.BlockSpec((1, tl), lambda j, q: (0, j)),
            pl.BlockSpec(memory_space=pl.ANY),
        ],
        out_specs=pl.BlockSpec((sb, tl), lambda j, q: (out_block(q), j)),
        out_shape=jax.ShapeDtypeStruct(y_flat.shape, F32),
        scratch_shapes=[pltpu.VMEM((CONV_HALO + ls, sb, tl), F32)],
        input_output_aliases={4: 0},
        compiler_params=_params("parallel", "arbitrary"),
        name="dwconv_sample",
    )(state_t, u_t, w, b.reshape(1, d), y_flat)


def _ln_pw2_kernel(y_ref, x_ref, lg_ref, lb_ref, w_ref, b_ref, o_ref):
    y = y_ref[...]
    mu = jnp.mean(y, axis=-1, keepdims=True)
    yc = y - mu
    var = jnp.mean(yc * yc, axis=-1, keepdims=True)
    z = _silu(yc * lax.rsqrt(var + EPS) * lg_ref[...] + lb_ref[...])
    o_ref[...] = x_ref[...] + _dot(z, w_ref[...]) + b_ref[...]


def _ln_pw2(y, x, lg, lb, w, b):
    t, d = x.shape
    dc = y.shape[1]
    tm = _proj_tile(t)
    return pl.pallas_call(
        _ln_pw2_kernel,
        grid=(t // tm,),
        in_specs=[
            pl.BlockSpec((tm, dc), lambda i: (i, 0)),
            pl.BlockSpec((tm, d), lambda i: (i, 0)),
            pl.BlockSpec((1, dc), lambda i: (0, 0)),
            pl.BlockSpec((1, dc), lambda i: (0, 0)),
            _resident((dc, d), lambda i: (0, 0)),
            pl.BlockSpec((1, d), lambda i: (0, 0)),
        ],
        out_specs=pl.BlockSpec((tm, d), lambda i: (i, 0)),
        out_shape=jax.ShapeDtypeStruct((t, d), F32),
        compiler_params=_params("parallel"),
        name="conv_ln_pw2",
    )(y, x, lg.reshape(1, dc), lb.reshape(1, dc), w, b.reshape(1, d))


def _log_sigmoid(z):
    return jnp.minimum(z, 0.0) - jnp.log1p(jnp.exp(-jnp.abs(z)))


def _norm_proj_kernel(x_ref, g_ref, *refs):
    *w_refs, p_ref = refs
    x = x_ref[...]
    xg = x * g_ref[...]
    rs = _row_scale(x)
    parts = [_dot(xg, w_ref[...]) * rs for w_ref in w_refs]
    p_ref[...] = parts[0] if len(parts) == 1 else jnp.concatenate(parts, axis=1)


def _norm_proj(x, g, ws, *, name):
    t, d = x.shape
    n = sum(w.shape[1] for w in ws)
    tm = _proj_tile(t)
    return pl.pallas_call(
        _norm_proj_kernel,
        grid=(t // tm,),
        in_specs=[pl.BlockSpec((tm, d), lambda i: (i, 0)), pl.BlockSpec((1, d), lambda i: (0, 0))]
        + [_resident(w.shape, lambda i: (0, 0)) for w in ws],
        out_specs=pl.BlockSpec((tm, n), lambda i: (i, 0)),
        out_shape=jax.ShapeDtypeStruct((t, n), F32),
        compiler_params=_params("parallel"),
        name=name,
    )(x, g.reshape(1, d), *ws)


def _gla_qk_decay_kernel(x_ref, g_ref, wq_ref, wk_ref, wg1_ref, wg2_ref, bg_ref, p_ref, la_ref):
    x = x_ref[...]
    xg = x * g_ref[...]
    rs = _row_scale(x)
    p_ref[...] = jnp.concatenate([_dot(xg, wq_ref[...]) * rs, _dot(xg, wk_ref[...]) * rs], axis=1)
    z = _dot(_dot(xg, wg1_ref[...]) * rs, wg2_ref[...]) + bg_ref[...]
    la_ref[...] = _log_sigmoid(z) * (1.0 / GLA_TAU)


def _gla_qk_decay(x, g, wq, wk, wg1, wg2, bg):
    t, d = x.shape
    r, nk = wg2.shape
    n = wq.shape[1] + wk.shape[1]
    tm = _proj_tile(t)
    return pl.pallas_call(
        _gla_qk_decay_kernel,
        grid=(t // tm,),
        in_specs=[
            pl.BlockSpec((tm, d), lambda i: (i, 0)),
            pl.BlockSpec((1, d), lambda i: (0, 0)),
            _resident(wq.shape, lambda i: (0, 0)),
            _resident(wk.shape, lambda i: (0, 0)),
            _resident((d, r), lambda i: (0, 0)),
            _resident((r, nk), lambda i: (0, 0)),
            pl.BlockSpec((1, nk), lambda i: (0, 0)),
        ],
        out_specs=[pl.BlockSpec((tm, n), lambda i: (i, 0)), pl.BlockSpec((tm, nk), lambda i: (i, 0))],
        out_shape=[jax.ShapeDtypeStruct((t, n), F32), jax.ShapeDtypeStruct((t, nk), F32)],
        compiler_params=_params("parallel"),
        name="gla_proj_qk_decay",
    )(x, g.reshape(1, d), wq, wk, wg1, wg2, bg.reshape(1, nk))


def _bf16_limbs(x):
    hi = x.astype(jnp.bfloat16).astype(F32)
    mid = (x - hi).astype(jnp.bfloat16).astype(F32)
    return hi, mid, x - hi - mid


def _gla_gate(o, r, gn):
    return o * lax.rsqrt(jnp.mean(o * o, axis=-1, keepdims=True) + EPS) * gn * _silu(r)


def _gla_chunk(q, k, v, la, s0):
    c, dk = q.shape
    n_sub = c // GLA_SUB
    neg_inf = -jnp.inf
    row = lax.broadcasted_iota(jnp.int32, (c, c), 0)
    col = lax.broadcasted_iota(jnp.int32, (c, c), 1)
    tri = jnp.where(row >= col, 1.0, 0.0).astype(F32)
    b = _dot(jnp.concatenate([tri, tri, tri], axis=1), jnp.concatenate(_bf16_limbs(la), axis=0)) * LOG2_E

    o = _dot(q * jnp.exp2(b), s0)

    row_c = lax.broadcasted_iota(jnp.int32, (c, 1), 0)
    lane_c = lax.broadcasted_iota(jnp.int32, (GLA_SUB, c), 1)
    t_sub = lax.broadcasted_iota(jnp.int32, (GLA_SUB, 1), 0)
    blocks = []
    for i in range(n_sub):
        lo = GLA_SUB * i
        qi, ki, bi = q[lo:lo + GLA_SUB], k[lo:lo + GLA_SUB], b[lo:lo + GLA_SUB]
        if i == 0:
            sc = jnp.zeros((GLA_SUB, c), F32)
        else:
            b_start = b[lo - 1:lo]
            qt = qi * jnp.exp2(bi - b_start)
            kt = k * jnp.exp2(jnp.where(row_c < lo, b_start - b, neg_inf))
            sc = lax.dot_general(qt, kt, (((1,), (1,)), ((), ())), preferred_element_type=F32)
        for s in range(GLA_SUB):
            colv = jnp.sum(qi * jnp.exp2(bi - bi[s:s + 1]) * ki[s:s + 1], axis=-1, keepdims=True)
            sc = sc + jnp.where(lane_c == lo + s, jnp.where(t_sub >= s, colv, 0.0), 0.0)
        blocks.append(sc)
    scores = blocks[0] if n_sub == 1 else jnp.concatenate(blocks, axis=0)
    o = o + _dot(scores, v)

    b_last = b[c - 1:c]
    kd = k * jnp.exp2(b_last - b)
    upd = lax.dot_general(kd, v, (((0,), (0,)), ((), ())), preferred_element_type=F32)
    s_new = _column_scale(jnp.exp2(b_last), s0) + upd
    return o, s_new


def _column_scale(d_row, s):
    dk, dv = s.shape
    d_col = jnp.transpose(jnp.broadcast_to(d_row, (128, dk)))
    return jnp.concatenate([d_col] * (dv // 128), axis=1) * s


def _gla_prompt_kernel(q_ref, k_ref, v_ref, r_ref, la_ref, gn_ref, og_ref, sn_ref, s_ref):
    rb = q_ref.shape[0]
    hps, dk, dv = s_ref.shape
    q_scale = dk ** -0.5
    j = pl.program_id(2)

    @pl.when(j == 0)
    def _():
        s_ref[...] = jnp.zeros_like(s_ref)

    def run(r0, c):
        rows = pl.ds(r0, c)
        for h in range(hps):
            kc, vc = slice(h * dk, (h + 1) * dk), slice(h * dv, (h + 1) * dv)
            o, s_new = _gla_chunk(q_ref[rows, kc] * q_scale, k_ref[rows, kc], v_ref[rows, vc], la_ref[rows, kc],
                                  s_ref[h])
            s_ref[h] = s_new
            og_ref[rows, vc] = _gla_gate(o, r_ref[rows, vc], gn_ref[...])

    n_full = rb // GLA_CHUNK

    def body(i, carry):
        run(pl.multiple_of(i * GLA_CHUNK, GLA_CHUNK), GLA_CHUNK)
        return carry

    lax.fori_loop(0, n_full, body, 0)
    if rb > n_full * GLA_CHUNK:
        run(n_full * GLA_CHUNK, rb - n_full * GLA_CHUNK)

    @pl.when(j == pl.num_programs(2) - 1)
    def _():
        sn_ref[0] = s_ref[...]


def _gla_prompt(qk, v, r, la, gn, *, n_seq, seq_len):
    h, hps = GLA_HEADS, GLA_HEADS_PER_STEP
    dk = la.shape[1] // h
    dv = gn.shape[0]
    rb = max(r_ for r_ in range(GLA_SUB, 1025, GLA_SUB) if seq_len % r_ == 0)
    nrb, ng = seq_len // rb, h // hps
    assert dv == 2 * dk
    return pl.pallas_call(
        _gla_prompt_kernel,
        grid=(n_seq, ng, nrb),
        in_specs=[
            pl.BlockSpec((rb, hps * dk), lambda b, g, j: (b * nrb + j, g)),
            pl.BlockSpec((rb, hps * dk), lambda b, g, j: (b * nrb + j, ng + g)),
            pl.BlockSpec((rb, hps * dv), lambda b, g, j: (b * nrb + j, g)),
            pl.BlockSpec((rb, hps * dv), lambda b, g, j: (b * nrb + j, g)),
            pl.BlockSpec((rb, hps * dk), lambda b, g, j: (b * nrb + j, g)),
            pl.BlockSpec((1, dv), lambda b, g, j: (0, 0)),
        ],
        out_specs=[
            pl.BlockSpec((rb, hps * dv), lambda b, g, j: (b * nrb + j, g)),
            pl.BlockSpec((1, hps, dk, dv), lambda b, g, j: (b, g, 0, 0)),
        ],
        out_shape=[jax.ShapeDtypeStruct(v.shape, F32),
                   jax.ShapeDtypeStruct((n_seq, h, dk, dv), F32)],
        scratch_shapes=[pltpu.VMEM((hps, dk, dv), F32)],
        compiler_params=_params("parallel", "parallel", "arbitrary"),
        name="gla_prompt",
    )(qk, qk, v, r, la, gn.reshape(1, dv))


def _gla_sample_kernel(p_ref, la_ref, s_ref, gn_ref, og_ref, sn_ref, *, seq_len):
    rows = p_ref.shape[0]
    n_seq = rows // seq_len
    dk = la_ref.shape[1] // GLA_HEADS
    dv = 2 * dk
    q_scale = dk ** -0.5
    neg_inf = -jnp.inf
    t_row = lax.broadcasted_iota(jnp.int32, (rows, 1), 0)
    for hd in range(GLA_HEADS):
        q = p_ref[:, hd * dk:(hd + 1) * dk] * q_scale
        k = p_ref[:, (GLA_HEADS + hd) * dk:(GLA_HEADS + hd + 1) * dk]
        v = p_ref[:, (GLA_HEADS + hd) * dv:(GLA_HEADS + hd + 1) * dv]
        r = p_ref[:, (2 * GLA_HEADS + hd) * dv:(2 * GLA_HEADS + hd + 1) * dv]
        la = la_ref[:, hd * dk:(hd + 1) * dk]

        def in_seq_from(s):
            return (t_row >= s) & (t_row < (s // seq_len + 1) * seq_len)

        b = jnp.zeros_like(la)
        for s in range(rows):
            b = b + jnp.where(in_seq_from(s), la[s:s + 1], 0.0)

        qe = q * jnp.exp(b)
        o = jnp.zeros((rows, dv), F32)
        for g in range(n_seq):
            mine = (t_row >= g * seq_len) & (t_row < (g + 1) * seq_len)
            o = jnp.where(mine, _dot(qe, s_ref[g, hd]), o)
        for s in range(rows):
            dec = jnp.exp(jnp.where(in_seq_from(s), b - b[s:s + 1], neg_inf))
            colv = jnp.sum(q * dec * k[s:s + 1], axis=-1, keepdims=True)
            o = o + colv * v[s:s + 1]
        og_ref[:, hd * dv:(hd + 1) * dv] = _gla_gate(o, r, gn_ref[...])

        for g in range(n_seq):
            last = (g + 1) * seq_len - 1
            b_last = b[last:last + 1]
            mine = (t_row >= g * seq_len) & (t_row < (g + 1) * seq_len)
            kd = k * jnp.exp(jnp.where(mine, b_last - b, neg_inf))
            upd = lax.dot_general(kd, v, (((0,), (0,)), ((), ())), preferred_element_type=F32)
            sn_ref[g, hd] = _column_scale(jnp.exp(b_last), s_ref[g, hd]) + upd


def _gla_sample(p_s, la_s, state, gn, *, seq_len):
    rows = 8
    n, h, dk, dv = state.shape
    t = p_s.shape[0]
    per = rows // seq_len
    return pl.pallas_call(
        functools.partial(_gla_sample_kernel, seq_len=seq_len),
        grid=(t // rows,),
        in_specs=[
            pl.BlockSpec((rows, p_s.shape[1]), lambda i: (i, 0)),
            pl.BlockSpec((rows, la_s.shape[1]), lambda i: (i, 0)),
            pl.BlockSpec((per, h, dk, dv), lambda i: (i, 0, 0, 0)),
            pl.BlockSpec((1, dv), lambda i: (0, 0)),
        ],
        out_specs=[
            pl.BlockSpec((rows, h * dv), lambda i: (i, 0)),
            pl.BlockSpec((per, h, dk, dv), lambda i: (i, 0, 0, 0)),
        ],
        out_shape=[jax.ShapeDtypeStruct((t, h * dv), F32), jax.ShapeDtypeStruct(state.shape, F32)],
        compiler_params=_params("parallel"),
        name="gla_sample",
    )(p_s, la_s, state, gn.reshape(1, dv))


def _out_proj_kernel(a_ref, x_ref, w_ref, o_ref):
    o_ref[...] = x_ref[...] + _dot(a_ref[...], w_ref[...])


def _out_proj(a, x, w):
    t, d = x.shape
    kdim = a.shape[1]
    tm = _proj_tile(t)
    return pl.pallas_call(
        _out_proj_kernel,
        grid=(t // tm,),
        in_specs=[
            pl.BlockSpec((tm, kdim), lambda i: (i, 0)),
            pl.BlockSpec((tm, d), lambda i: (i, 0)),
            _resident((kdim, d), lambda i: (0, 0)),
        ],
        out_specs=pl.BlockSpec((tm, d), lambda i: (i, 0)),
        out_shape=jax.ShapeDtypeStruct((t, d), F32),
        compiler_params=_params("parallel"),
        name="gla_out_proj",
    )(a, x, w)


def kernel(x_prompt, x_sample, state_conv, state_gla, meta_tokens, norm_ffn, w_ffn_gate, w_ffn_up, w_ffn_down,
           norm_mix, conv_w_pw1, conv_b_pw1, conv_w_dw, conv_b_dw, conv_ln_g, conv_ln_b, conv_w_pw2, conv_b_pw2,
           gla_w_q, gla_w_k, gla_w_v, gla_w_g1, gla_w_g2, gla_b_g, gla_w_r, gla_gn_g, gla_w_o, norm_final):
    bp, sp, d = x_prompt.shape
    bs, ls, _ = x_sample.shape
    lp = N_META_ROWS + sp
    tp, ts = bp * lp, bs * ls
    n_pad = -(tp + ts) % TOKEN_ROW_MULTIPLE
    assert 8 % ls == 0 and ts % 8 == 0

    def with_pad(a):
        return jnp.concatenate([a, jnp.zeros((n_pad, a.shape[1]), F32)], axis=0) if n_pad else a

    x = jnp.concatenate(
        [piece for b in range(bp) for piece in (meta_tokens, x_prompt[b])]
        + [with_pad(jnp.transpose(x_sample, (1, 0, 2)).reshape(ts, d))], axis=0)

    def ffn(x, i, j, final_norm=False):
        return _ffn(x, norm_ffn, w_ffn_gate, w_ffn_up, w_ffn_down, norm_final,
                    layer=i, slot=j, final_norm=final_norm)

    def prompt_rows(a, start):
        return jnp.stack([a[b * lp + start:(b + 1) * lp] for b in range(bp)])

    def sample_rows(a):
        return a[tp:tp + ts].reshape(ls, bs, -1)

    x = ffn(x, 0, 0)
    u = _pw1_glu(x, norm_mix[0], conv_w_pw1[0], conv_b_pw1[0])
    u_s = sample_rows(u)
    y = _dwconv_prompt(u, conv_w_dw[0], conv_b_dw[0], n_seq=bp, seq_len=lp)
    y = _dwconv_sample(y, jnp.transpose(state_conv[0], (1, 0, 2)), u_s, conv_w_dw[0], conv_b_dw[0], row0=tp)
    x = _ln_pw2(y, x, conv_ln_g[0], conv_ln_b[0], conv_w_pw2[0], conv_b_pw2[0])
    x = ffn(x, 0, 1)
    new_conv_prompt = prompt_rows(u, lp - CONV_HALO)[None]
    new_conv_sample = jnp.concatenate([state_conv[0], jnp.transpose(u_s, (1, 0, 2))], axis=1)[None, :, ls:]

    x = ffn(x, 1, 0)
    p_qk, la = _gla_qk_decay(x, norm_mix[1], gla_w_q[0], gla_w_k[0], gla_w_g1[0], gla_w_g2[0], gla_b_g[0])
    p_v = _norm_proj(x, norm_mix[1], [gla_w_v[0]], name="gla_proj_v")
    p_r = _norm_proj(x, norm_mix[1], [gla_w_r[0]], name="gla_proj_r")
    og, new_gla_prompt = _gla_prompt(p_qk, p_v, p_r, la, gla_gn_g[0], n_seq=bp, seq_len=lp)

    def seq_major(a):
        return jnp.transpose(sample_rows(a), (1, 0, 2)).reshape(ts, -1)

    p_s = jnp.concatenate([seq_major(p_qk), seq_major(p_v), seq_major(p_r)], axis=1)
    og_s, new_gla_sample = _gla_sample(p_s, seq_major(la), state_gla[0], gla_gn_g[0], seq_len=ls)
    og_s = jnp.transpose(og_s.reshape(bs, ls, -1), (1, 0, 2)).reshape(ts, -1)
    og = lax.dynamic_update_slice(og, with_pad(og_s), (tp, 0))
    x = _out_proj(og, x, gla_w_o[0])
    x = ffn(x, 1, 1, final_norm=True)

    y_prompt = prompt_rows(x, N_META_ROWS)
    y_sample = jnp.transpose(sample_rows(x), (1, 0, 2))
    return (y_prompt, y_sample, new_conv_prompt, new_gla_prompt[None], new_conv_sample, new_gla_sample[None])
```

```python
import functools
import math

import jax
import jax.numpy as jnp
from jax import lax
from jax.experimental import pallas as pl
from jax.experimental.pallas import tpu as pltpu

F32 = jnp.float32
EPS = 1e-6
N_META_ROWS = 16
CONV_TAPS = 31
CONV_HALO = CONV_TAPS - 1
HALO_PAD = 32
GLA_HEADS = 4
GLA_HEADS_PER_STEP = 4
GLA_SUB = 16
GLA_CHUNK = 64
GLA_TAU = 16.0
LOG2_E = 1.4426950408889634
V7X_VMEM_LIMIT = 60000 * 1024

TOKEN_ROW_MULTIPLE = 128
MAX_TOKEN_TILE = 1152
FFN_TILE = 256
PROJ_GROUP = 2048
CONV_LANE_TILE = 256
CONV_SAMPLE_LANE_TILE = 512


def _token_tile(t_pad):
    nb = -(-t_pad // MAX_TOKEN_TILE)
    assert t_pad % (16 * nb) == 0, t_pad
    return t_pad // nb


def _proj_tile(t_pad):
    return _token_tile(t_pad) // 2


def _params(*sem):
    return pltpu.CompilerParams(dimension_semantics=sem, vmem_limit_bytes=V7X_VMEM_LIMIT)


def _resident(shape, index_map):
    return pl.BlockSpec(shape, index_map, pipeline_mode=pl.Buffered(1))


def _rms(x, g):
    return x * lax.rsqrt(jnp.mean(x * x, axis=-1, keepdims=True) + EPS) * g


def _silu(x):
    return x * jax.nn.sigmoid(x)


def _dot(a, b):
    return jnp.dot(a, b, preferred_element_type=F32)


def _ffn_kernel(x_ref, g_ref, wg_ref, wu_ref, wd_ref, gf_ref, o_ref, h_ref, *, final_norm):
    f = pl.program_id(1)

    @pl.when(f == 0)
    def _():
        x = x_ref[...]
        h_ref[...] = _rms(x, g_ref[...])
        o_ref[...] = x

    h = h_ref[...]
    a = 0.5 * _silu(_dot(h, wg_ref[...])) * _dot(h, wu_ref[...])
    o_ref[...] += _dot(a, wd_ref[...])

    if final_norm:
        @pl.when(f == pl.num_programs(1) - 1)
        def _():
            o_ref[...] = _rms(o_ref[...], gf_ref[...])


def _ffn(x, g, wg, wu, wd, gf, *, layer, slot, final_norm):
    t, d = x.shape
    f = wg.shape[-1]
    tm, tf = _token_tile(t), FFN_TILE
    return pl.pallas_call(
        functools.partial(_ffn_kernel, final_norm=final_norm),
        grid=(t // tm, f // tf),
        in_specs=[
            pl.BlockSpec((tm, d), lambda i, j: (i, 0)),
            pl.BlockSpec((None, None, 1, d), lambda i, j: (layer, slot, 0, 0)),
            pl.BlockSpec((None, None, d, tf), lambda i, j: (layer, slot, 0, j)),
            pl.BlockSpec((None, None, d, tf), lambda i, j: (layer, slot, 0, j)),
            pl.BlockSpec((None, None, tf, d), lambda i, j: (layer, slot, j, 0)),
            pl.BlockSpec((1, d), lambda i, j: (0, 0)),
        ],
        out_specs=pl.BlockSpec((tm, d), lambda i, j: (i, 0)),
        out_shape=jax.ShapeDtypeStruct((t, d), F32),
        scratch_shapes=[pltpu.VMEM((tm, d), F32)],
        compiler_params=_params("parallel", "arbitrary"),
        name="ffn_final" if final_norm else "ffn",
    )(x, g.reshape(g.shape[0], g.shape[1], 1, d), wg, wu, wd, gf.reshape(1, d))


def _row_scale(x):
    return lax.rsqrt(jnp.mean(x * x, axis=-1, keepdims=True) + EPS)


def _pw1_kernel(x_ref, g_ref, wa_ref, wb_ref, ba_ref, bb_ref, u_ref):
    x = x_ref[...]
    xg = x * g_ref[...]
    rs = _row_scale(x)
    a = _dot(xg, wa_ref[...]) * rs + ba_ref[...]
    b = _dot(xg, wb_ref[...]) * rs + bb_ref[...]
    u_ref[...] = a * jax.nn.sigmoid(b)


def _pw1_glu(x, g, w, b):
    t, d = x.shape
    dc = w.shape[1] // 2
    tm, tn = _proj_tile(t), PROJ_GROUP // 2
    nc = dc // tn
    b2 = b.reshape(1, 2 * dc)
    return pl.pallas_call(
        _pw1_kernel,
        grid=(nc, t // tm),
        in_specs=[
            pl.BlockSpec((tm, d), lambda c, i: (i, 0)),
            pl.BlockSpec((1, d), lambda c, i: (0, 0)),
            _resident((d, tn), lambda c, i: (0, c)),
            _resident((d, tn), lambda c, i: (0, c + nc)),
            pl.BlockSpec((1, tn), lambda c, i: (0, c)),
            pl.BlockSpec((1, tn), lambda c, i: (0, c + nc)),
        ],
        out_specs=pl.BlockSpec((tm, tn), lambda c, i: (i, c)),
        out_shape=jax.ShapeDtypeStruct((t, dc), F32),
        compiler_params=_params("arbitrary", "arbitrary"),
        name="conv_pw1_glu",
    )(x, g.reshape(1, d), w, w, b2, b2)


def _conv_window(win, wb_ref, b_ref, rc):
    lead = HALO_PAD - CONV_HALO
    tl = win.shape[1]
    n_tiles = rc // 8 + 1
    acc = jnp.broadcast_to(b_ref[...], (rc, tl))
    for s in range(8):
        part = None
        for a in range((CONV_TAPS + lead + 7) // 8):
            k = 8 * a + s - lead
            if 0 <= k < CONV_TAPS:
                term = (win[8 * a:8 * a + rc + 8].reshape(n_tiles, 8, tl) * wb_ref[k][None]).reshape(rc + 8, tl)
                part = term if part is None else part + term
        acc = acc + (part[:rc] if s == 0 else pltpu.roll(part, rc + 8 - s, axis=0)[:rc])
    return acc


def _dwconv_prompt_kernel(cur_ref, w_ref, b_ref, y_ref, ext_ref, wb_ref, *, rc):
    lp, tl = cur_ref.shape
    ext_ref[0:HALO_PAD, :] = jnp.zeros((HALO_PAD, tl), F32)
    ext_ref[HALO_PAD:HALO_PAD + lp, :] = cur_ref[...]
    ext_ref[HALO_PAD + lp:HALO_PAD + lp + 8, :] = jnp.zeros((8, tl), F32)
    for k in range(CONV_TAPS):
        wb_ref[k] = jnp.broadcast_to(w_ref[k:k + 1, :], (8, tl))

    def body(c, carry):
        r0 = pl.multiple_of(c * rc, 8)
        y_ref[pl.ds(r0, rc), :] = _conv_window(ext_ref[pl.ds(r0, rc + HALO_PAD + 8), :], wb_ref, b_ref, rc)
        return carry

    lax.fori_loop(0, lp // rc, body, 0)


def _dwconv_prompt(u, w, b, *, n_seq, seq_len):
    d = u.shape[1]
    tl = CONV_LANE_TILE
    rc = max(r for r in range(8, 65, 8) if seq_len % r == 0)
    return pl.pallas_call(
        functools.partial(_dwconv_prompt_kernel, rc=rc),
        grid=(n_seq, d // tl),
        in_specs=[
            pl.BlockSpec((seq_len, tl), lambda i, j: (i, j)),
            pl.BlockSpec((CONV_TAPS, tl), lambda i, j: (0, j)),
            pl.BlockSpec((1, tl), lambda i, j: (0, j)),
        ],
        out_specs=pl.BlockSpec((seq_len, tl), lambda i, j: (i, j)),
        out_shape=jax.ShapeDtypeStruct(u.shape, F32),
        scratch_shapes=[pltpu.VMEM((HALO_PAD + seq_len + 8, tl), F32), pltpu.VMEM((CONV_TAPS, 8, tl), F32)],
        compiler_params=_params("parallel", "parallel"),
        name="dwconv_prompt",
    )(u, w, b.reshape(1, d))


def _dwconv_sample_kernel(st_ref, u_ref, w_ref, b_ref, y_hbm_ref, y_ref, ext_ref, *, n_conv_steps):
    del y_hbm_ref
    ls = u_ref.shape[0]
    q = pl.program_id(1)
    t = q % ls

    @pl.when((t == 0) & (q < n_conv_steps))
    def _():
        ext_ref[0:CONV_HALO] = st_ref[...]
        ext_ref[CONV_HALO:CONV_HALO + ls] = u_ref[...]

    @pl.when(q < n_conv_steps)
    def _():
        acc = jnp.broadcast_to(b_ref[...], y_ref.shape)
        for k in range(CONV_TAPS):
            acc = acc + w_ref[k:k + 1, :] * ext_ref[t + k]
        y_ref[...] = acc

    @pl.when(q >= n_conv_steps)
    def _():
        y_ref[...] = jnp.zeros_like(y_ref)


def _dwconv_sample(y_flat, state_t, u_t, w, b, *, row0):
    ls, bs, d = u_t.shape
    tl = CONV_SAMPLE_LANE_TILE
    sb = math.gcd(row0, bs)
    nsb = bs // sb
    n_pad = y_flat.shape[0] - row0 - ls * bs
    assert sb % 8 == 0 and n_pad % sb == 0
    n_conv = ls * nsb

    def seq_block(q):
        return jnp.minimum(q // ls, nsb - 1)

    def out_block(q):
        return row0 // sb + jnp.where(q < n_conv, (q % ls) * nsb + q // ls, q)

    return pl.pallas_call(
        functools.partial(_dwconv_sample_kernel, n_conv_steps=n_conv),
        grid=(d // tl, n_conv + n_pad // sb),
        in_specs=[
            pl.BlockSpec((CONV_HALO, sb, tl), lambda j, q: (0, seq_block(q), j)),
            pl.BlockSpec((ls, sb, tl), lambda j, q: (0, seq_block(q), j)),
            pl.BlockSpec((CONV_TAPS, tl), lambda j, q: (0, j)),
            pl.BlockSpec((1, tl), lambda j, q: (0, j)),
            pl.BlockSpec(memory_space=pl.ANY),
        ],
        out_specs=pl.BlockSpec((sb, tl), lambda j, q: (out_block(q), j)),
        out_shape=jax.ShapeDtypeStruct(y_flat.shape, F32),
        scratch_shapes=[pltpu.VMEM((CONV_HALO + ls, sb, tl), F32)],
        input_output_aliases={4: 0},
        compiler_params=_params("parallel", "arbitrary"),
        name="dwconv_sample",
    )(state_t, u_t, w, b.reshape(1, d), y_flat)


def _ln_pw2_kernel(y_ref, x_ref, lg_ref, lb_ref, w_ref, b_ref, o_ref):
    y = y_ref[...]
    mu = jnp.mean(y, axis=-1, keepdims=True)
    yc = y - mu
    var = jnp.mean(yc * yc, axis=-1, keepdims=True)
    z = _silu(yc * lax.rsqrt(var + EPS) * lg_ref[...] + lb_ref[...])
    o_ref[...] = x_ref[...] + _dot(z, w_ref[...]) + b_ref[...]


def _ln_pw2(y, x, lg, lb, w, b):
    t, d = x.shape
    dc = y.shape[1]
    tm = _proj_tile(t)
    return pl.pallas_call(
        _ln_pw2_kernel,
        grid=(t // tm,),
        in_specs=[
            pl.BlockSpec((tm, dc), lambda i: (i, 0)),
            pl.BlockSpec((tm, d), lambda i: (i, 0)),
            pl.BlockSpec((1, dc), lambda i: (0, 0)),
            pl.BlockSpec((1, dc), lambda i: (0, 0)),
            _resident((dc, d), lambda i: (0, 0)),
            pl.BlockSpec((1, d), lambda i: (0, 0)),
        ],
        out_specs=pl.BlockSpec((tm, d), lambda i: (i, 0)),
        out_shape=jax.ShapeDtypeStruct((t, d), F32),
        compiler_params=_params("parallel"),
        name="conv_ln_pw2",
    )(y, x, lg.reshape(1, dc), lb.reshape(1, dc), w, b.reshape(1, d))


def _log_sigmoid(z):
    return jnp.minimum(z, 0.0) - jnp.log1p(jnp.exp(-jnp.abs(z)))


def _norm_proj_kernel(x_ref, g_ref, *refs):
    *w_refs, p_ref = refs
    x = x_ref[...]
    xg = x * g_ref[...]
    rs = _row_scale(x)
    parts = [_dot(xg, w_ref[...]) * rs for w_ref in w_refs]
    p_ref[...] = parts[0] if len(parts) == 1 else jnp.concatenate(parts, axis=1)


def _norm_proj(x, g, ws, *, name):
    t, d = x.shape
    n = sum(w.shape[1] for w in ws)
    tm = _proj_tile(t)
    return pl.pallas_call(
        _norm_proj_kernel,
        grid=(t // tm,),
        in_specs=[pl.BlockSpec((tm, d), lambda i: (i, 0)), pl.BlockSpec((1, d), lambda i: (0, 0))]
        + [_resident(w.shape, lambda i: (0, 0)) for w in ws],
        out_specs=pl.BlockSpec((tm, n), lambda i: (i, 0)),
        out_shape=jax.ShapeDtypeStruct((t, n), F32),
        compiler_params=_params("parallel"),
        name=name,
    )(x, g.reshape(1, d), *ws)


def _gla_qk_decay_kernel(x_ref, g_ref, wq_ref, wk_ref, wg1_ref, wg2_ref, bg_ref, p_ref, la_ref):
    x = x_ref[...]
    xg = x * g_ref[...]
    rs = _row_scale(x)
    p_ref[...] = jnp.concatenate([_dot(xg, wq_ref[...]) * rs, _dot(xg, wk_ref[...]) * rs], axis=1)
    z = _dot(_dot(xg, wg1_ref[...]) * rs, wg2_ref[...]) + bg_ref[...]
    la_ref[...] = _log_sigmoid(z) * (1.0 / GLA_TAU)


def _gla_qk_decay(x, g, wq, wk, wg1, wg2, bg):
    t, d = x.shape
    r, nk = wg2.shape
    n = wq.shape[1] + wk.shape[1]
    tm = _proj_tile(t)
    return pl.pallas_call(
        _gla_qk_decay_kernel,
        grid=(t // tm,),
        in_specs=[
            pl.BlockSpec((tm, d), lambda i: (i, 0)),
            pl.BlockSpec((1, d), lambda i: (0, 0)),
            _resident(wq.shape, lambda i: (0, 0)),
            _resident(wk.shape, lambda i: (0, 0)),
            _resident((d, r), lambda i: (0, 0)),
            _resident((r, nk), lambda i: (0, 0)),
            pl.BlockSpec((1, nk), lambda i: (0, 0)),
        ],
        out_specs=[pl.BlockSpec((tm, n), lambda i: (i, 0)), pl.BlockSpec((tm, nk), lambda i: (i, 0))],
        out_shape=[jax.ShapeDtypeStruct((t, n), F32), jax.ShapeDtypeStruct((t, nk), F32)],
        compiler_params=_params("parallel"),
        name="gla_proj_qk_decay",
    )(x, g.reshape(1, d), wq, wk, wg1, wg2, bg.reshape(1, nk))


def _bf16_limbs(x):
    hi = x.astype(jnp.bfloat16).astype(F32)
    mid = (x - hi).astype(jnp.bfloat16).astype(F32)
    return hi, mid, x - hi - mid


def _gla_gate(o, r, gn):
    return o * lax.rsqrt(jnp.mean(o * o, axis=-1, keepdims=True) + EPS) * gn * _silu(r)


def _gla_chunk(q, k, v, la, s0):
    c, dk = q.shape
    n_sub = c // GLA_SUB
    neg_inf = -jnp.inf
    row = lax.broadcasted_iota(jnp.int32, (c, c), 0)
    col = lax.broadcasted_iota(jnp.int32, (c, c), 1)
    tri = jnp.where(row >= col, 1.0, 0.0).astype(F32)
    b = _dot(jnp.concatenate([tri, tri, tri], axis=1), jnp.concatenate(_bf16_limbs(la), axis=0)) * LOG2_E

    o = _dot(q * jnp.exp2(b), s0)

    row_c = lax.broadcasted_iota(jnp.int32, (c, 1), 0)
    lane_c = lax.broadcasted_iota(jnp.int32, (GLA_SUB, c), 1)
    t_sub = lax.broadcasted_iota(jnp.int32, (GLA_SUB, 1), 0)
    blocks = []
    for i in range(n_sub):
        lo = GLA_SUB * i
        qi, ki, bi = q[lo:lo + GLA_SUB], k[lo:lo + GLA_SUB], b[lo:lo + GLA_SUB]
        if i == 0:
            sc = jnp.zeros((GLA_SUB, c), F32)
        else:
            b_start = b[lo - 1:lo]
            qt = qi * jnp.exp2(bi - b_start)
            kt = k * jnp.exp2(jnp.where(row_c < lo, b_start - b, neg_inf))
            sc = lax.dot_general(qt, kt, (((1,), (1,)), ((), ())), preferred_element_type=F32)
        for s in range(GLA_SUB):
            colv = jnp.sum(qi * jnp.exp2(bi - bi[s:s + 1]) * ki[s:s + 1], axis=-1, keepdims=True)
            sc = sc + jnp.where(lane_c == lo + s, jnp.where(t_sub >= s, colv, 0.0), 0.0)
        blocks.append(sc)
    scores = blocks[0] if n_sub == 1 else jnp.concatenate(blocks, axis=0)
    o = o + _dot(scores, v)

    b_last = b[c - 1:c]
    kd = k * jnp.exp2(b_last - b)
    upd = lax.dot_general(kd, v, (((0,), (0,)), ((), ())), preferred_element_type=F32)
    s_new = _column_scale(jnp.exp2(b_last), s0) + upd
    return o, s_new


def _column_scale(d_row, s):
    dk, dv = s.shape
    d_col = jnp.transpose(jnp.broadcast_to(d_row, (128, dk)))
    return jnp.concatenate([d_col] * (dv // 128), axis=1) * s


def _gla_prompt_kernel(q_ref, k_ref, v_ref, r_ref, la_ref, gn_ref, og_ref, sn_ref, s_ref):
    rb = q_ref.shape[0]
    hps, dk, dv = s_ref.shape
    q_scale = dk ** -0.5
    j = pl.program_id(2)

    @pl.when(j == 0)
    def _():
        s_ref[...] = jnp.zeros_like(s_ref)

    def run(r0, c):
        rows = pl.ds(r0, c)
        for h in range(hps):
            kc, vc = slice(h * dk, (h + 1) * dk), slice(h * dv, (h + 1) * dv)
            o, s_new = _gla_chunk(q_ref[rows, kc] * q_scale, k_ref[rows, kc], v_ref[rows, vc], la_ref[rows, kc],
                                  s_ref[h])
            s_ref[h] = s_new
            og_ref[rows, vc] = _gla_gate(o, r_ref[rows, vc], gn_ref[...])

    n_full = rb // GLA_CHUNK

    def body(i, carry):
        run(pl.multiple_of(i * GLA_CHUNK, GLA_CHUNK), GLA_CHUNK)
        return carry

    lax.fori_loop(0, n_full, body, 0)
    if rb > n_full * GLA_CHUNK:
        run(n_full * GLA_CHUNK, rb - n_full * GLA_CHUNK)

    @pl.when(j == pl.num_programs(2) - 1)
    def _():
        sn_ref[0] = s_ref[...]


def _gla_prompt(qk, v, r, la, gn, *, n_seq, seq_len):
    h, hps = GLA_HEADS, GLA_HEADS_PER_STEP
    dk = la.shape[1] // h
    dv = gn.shape[0]
    rb = max(r_ for r_ in range(GLA_SUB, 1025, GLA_SUB) if seq_len % r_ == 0)
    nrb, ng = seq_len // rb, h // hps
    assert dv == 2 * dk
    return pl.pallas_call(
        _gla_prompt_kernel,
        grid=(n_seq, ng, nrb),
        in_specs=[
            pl.BlockSpec((rb, hps * dk), lambda b, g, j: (b * nrb + j, g)),
            pl.BlockSpec((rb, hps * dk), lambda b, g, j: (b * nrb + j, ng + g)),
            pl.BlockSpec((rb, hps * dv), lambda b, g, j: (b * nrb + j, g)),
            pl.BlockSpec((rb, hps * dv), lambda b, g, j: (b * nrb + j, g)),
            pl.BlockSpec((rb, hps * dk), lambda b, g, j: (b * nrb + j, g)),
            pl.BlockSpec((1, dv), lambda b, g, j: (0, 0)),
        ],
        out_specs=[
            pl.BlockSpec((rb, hps * dv), lambda b, g, j: (b * nrb + j, g)),
            pl.BlockSpec((1, hps, dk, dv), lambda b, g, j: (b, g, 0, 0)),
        ],
        out_shape=[jax.ShapeDtypeStruct(v.shape, F32),
                   jax.ShapeDtypeStruct((n_seq, h, dk, dv), F32)],
        scratch_shapes=[pltpu.VMEM((hps, dk, dv), F32)],
        compiler_params=_params("parallel", "parallel", "arbitrary"),
        name="gla_prompt",
    )(qk, qk, v, r, la, gn.reshape(1, dv))


def _gla_sample_kernel(p_ref, la_ref, s_ref, gn_ref, og_ref, sn_ref, *, seq_len):
    rows = p_ref.shape[0]
    n_seq = rows // seq_len
    dk = la_ref.shape[1] // GLA_HEADS
    dv = 2 * dk
    q_scale = dk ** -0.5
    neg_inf = -jnp.inf
    t_row = lax.broadcasted_iota(jnp.int32, (rows, 1), 0)
    for hd in range(GLA_HEADS):
        q = p_ref[:, hd * dk:(hd + 1) * dk] * q_scale
        k = p_ref[:, (GLA_HEADS + hd) * dk:(GLA_HEADS + hd + 1) * dk]
        v = p_ref[:, (GLA_HEADS + hd) * dv:(GLA_HEADS + hd + 1) * dv]
        r = p_ref[:, (2 * GLA_HEADS + hd) * dv:(2 * GLA_HEADS + hd + 1) * dv]
        la = la_ref[:, hd * dk:(hd + 1) * dk]

        def in_seq_from(s):
            return (t_row >= s) & (t_row < (s // seq_len + 1) * seq_len)

        b = jnp.zeros_like(la)
        for s in range(rows):
            b = b + jnp.where(in_seq_from(s), la[s:s + 1], 0.0)

        qe = q * jnp.exp(b)
        o = jnp.zeros((rows, dv), F32)
        for g in range(n_seq):
            mine = (t_row >= g * seq_len) & (t_row < (g + 1) * seq_len)
            o = jnp.where(mine, _dot(qe, s_ref[g, hd]), o)
        for s in range(rows):
            dec = jnp.exp(jnp.where(in_seq_from(s), b - b[s:s + 1], neg_inf))
            colv = jnp.sum(q * dec * k[s:s + 1], axis=-1, keepdims=True)
            o = o + colv * v[s:s + 1]
        og_ref[:, hd * dv:(hd + 1) * dv] = _gla_gate(o, r, gn_ref[...])

        for g in range(n_seq):
            last = (g + 1) * seq_len - 1
            b_last = b[last:last + 1]
            mine = (t_row >= g * seq_len) & (t_row < (g + 1) * seq_len)
            kd = k * jnp.exp(jnp.where(mine, b_last - b, neg_inf))
            upd = lax.dot_general(kd, v, (((0,), (0,)), ((), ())), preferred_element_type=F32)
            sn_ref[g, hd] = _column_scale(jnp.exp(b_last), s_ref[g, hd]) + upd


def _gla_sample(p_s, la_s, state, gn, *, seq_len):
    rows = 8
    n, h, dk, dv = state.shape
    t = p_s.shape[0]
    per = rows // seq_len
    return pl.pallas_call(
        functools.partial(_gla_sample_kernel, seq_len=seq_len),
        grid=(t // rows,),
        in_specs=[
            pl.BlockSpec((rows, p_s.shape[1]), lambda i: (i, 0)),
            pl.BlockSpec((rows, la_s.shape[1]), lambda i: (i, 0)),
            pl.BlockSpec((per, h, dk, dv), lambda i: (i, 0, 0, 0)),
            pl.BlockSpec((1, dv), lambda i: (0, 0)),
        ],
        out_specs=[
            pl.BlockSpec((rows, h * dv), lambda i: (i, 0)),
            pl.BlockSpec((per, h, dk, dv), lambda i: (i, 0, 0, 0)),
        ],
        out_shape=[jax.ShapeDtypeStruct((t, h * dv), F32), jax.ShapeDtypeStruct(state.shape, F32)],
        compiler_params=_params("parallel"),
        name="gla_sample",
    )(p_s, la_s, state, gn.reshape(1, dv))


def _out_proj_kernel(a_ref, x_ref, w_ref, o_ref):
    o_ref[...] = x_ref[...] + _dot(a_ref[...], w_ref[...])


def _out_proj(a, x, w):
    t, d = x.shape
    kdim = a.shape[1]
    tm = _proj_tile(t)
    return pl.pallas_call(
        _out_proj_kernel,
        grid=(t // tm,),
        in_specs=[
            pl.BlockSpec((tm, kdim), lambda i: (i, 0)),
            pl.BlockSpec((tm, d), lambda i: (i, 0)),
            _resident((kdim, d), lambda i: (0, 0)),
        ],
        out_specs=pl.BlockSpec((tm, d), lambda i: (i, 0)),
        out_shape=jax.ShapeDtypeStruct((t, d), F32),
        compiler_params=_params("parallel"),
        name="gla_out_proj",
    )(a, x, w)


def _prompt_rows_kernel(x_hbm, o_hbm, sems, *, seq_len, start):
    copies = [
        pltpu.make_async_copy(x_hbm.at[pl.ds(b * seq_len + start, seq_len - start)], o_hbm.at[b], sems.at[b])
        for b in range(o_hbm.shape[0])
    ]
    for copy in copies:
        copy.start()
    for copy in copies:
        copy.wait()


def _prompt_rows(x, *, n_seq, seq_len, start):
    assert start % 8 == 0 and seq_len % 8 == 0
    return pl.pallas_call(
        functools.partial(_prompt_rows_kernel, seq_len=seq_len, start=start),
        in_specs=[pl.BlockSpec(memory_space=pl.ANY)],
        out_specs=pl.BlockSpec(memory_space=pl.ANY),
        out_shape=jax.ShapeDtypeStruct((n_seq, seq_len - start, x.shape[1]), F32),
        scratch_shapes=[pltpu.SemaphoreType.DMA((n_seq,))],
        name="prompt_rows",
    )(x)


def kernel(x_prompt, x_sample, state_conv, state_gla, meta_tokens, norm_ffn, w_ffn_gate, w_ffn_up, w_ffn_down,
           norm_mix, conv_w_pw1, conv_b_pw1, conv_w_dw, conv_b_dw, conv_ln_g, conv_ln_b, conv_w_pw2, conv_b_pw2,
           gla_w_q, gla_w_k, gla_w_v, gla_w_g1, gla_w_g2, gla_b_g, gla_w_r, gla_gn_g, gla_w_o, norm_final):
    bp, sp, d = x_prompt.shape
    bs, ls, _ = x_sample.shape
    lp = N_META_ROWS + sp
    tp, ts = bp * lp, bs * ls
    n_pad = -(tp + ts) % TOKEN_ROW_MULTIPLE
    assert 8 % ls == 0 and ts % 8 == 0

    def with_pad(a):
        return jnp.concatenate([a, jnp.zeros((n_pad, a.shape[1]), F32)], axis=0) if n_pad else a

    x = jnp.concatenate(
        [piece for b in range(bp) for piece in (meta_tokens, x_prompt[b])]
        + [with_pad(jnp.transpose(x_sample, (1, 0, 2)).reshape(ts, d))], axis=0)

    def ffn(x, i, j, final_norm=False):
        return _ffn(x, norm_ffn, w_ffn_gate, w_ffn_up, w_ffn_down, norm_final,
                    layer=i, slot=j, final_norm=final_norm)

    def prompt_rows(a, start):
        return jnp.stack([a[b * lp + start:(b + 1) * lp] for b in range(bp)])

    def sample_rows(a):
        return a[tp:tp + ts].reshape(ls, bs, -1)

    x = ffn(x, 0, 0)
    u = _pw1_glu(x, norm_mix[0], conv_w_pw1[0], conv_b_pw1[0])
    u_s = sample_rows(u)
    y = _dwconv_prompt(u, conv_w_dw[0], conv_b_dw[0], n_seq=bp, seq_len=lp)
    y = _dwconv_sample(y, jnp.transpose(state_conv[0], (1, 0, 2)), u_s, conv_w_dw[0], conv_b_dw[0], row0=tp)
    x = _ln_pw2(y, x, conv_ln_g[0], conv_ln_b[0], conv_w_pw2[0], conv_b_pw2[0])
    x = ffn(x, 0, 1)
    new_conv_prompt = prompt_rows(u, lp - CONV_HALO)[None]
    new_conv_sample = jnp.concatenate([state_conv[0], jnp.transpose(u_s, (1, 0, 2))], axis=1)[None, :, ls:]

    x = ffn(x, 1, 0)
    p_qk, la = _gla_qk_decay(x, norm_mix[1], gla_w_q[0], gla_w_k[0], gla_w_g1[0], gla_w_g2[0], gla_b_g[0])
    p_v = _norm_proj(x, norm_mix[1], [gla_w_v[0]], name="gla_proj_v")
    p_r = _norm_proj(x, norm_mix[1], [gla_w_r[0]], name="gla_proj_r")
    og, new_gla_prompt = _gla_prompt(p_qk, p_v, p_r, la, gla_gn_g[0], n_seq=bp, seq_len=lp)

    def seq_major(a):
        return jnp.transpose(sample_rows(a), (1, 0, 2)).reshape(ts, -1)

    p_s = jnp.concatenate([seq_major(p_qk), seq_major(p_v), seq_major(p_r)], axis=1)
    og_s, new_gla_sample = _gla_sample(p_s, seq_major(la), state_gla[0], gla_gn_g[0], seq_len=ls)
    og_s = jnp.transpose(og_s.reshape(bs, ls, -1), (1, 0, 2)).reshape(ts, -1)
    og = lax.dynamic_update_slice(og, with_pad(og_s), (tp, 0))
    x = _out_proj(og, x, gla_w_o[0])
    x = ffn(x, 1, 1, final_norm=True)

    y_prompt = _prompt_rows(x, n_seq=bp, seq_len=lp, start=N_META_ROWS)
    y_sample = jnp.transpose(sample_rows(x), (1, 0, 2))
    return (y_prompt, y_sample, new_conv_prompt, new_gla_prompt[None], new_conv_sample, new_gla_sample[None])
```

```python
import functools
import math

import jax
import jax.numpy as jnp
from jax import lax
from jax.experimental import pallas as pl
from jax.experimental.pallas import tpu as pltpu

F32 = jnp.float32
EPS = 1e-6
N_META_ROWS = 16
CONV_TAPS = 31
CONV_HALO = CONV_TAPS - 1
HALO_PAD = 32
GLA_HEADS = 4
GLA_HEADS_PER_STEP = 4
GLA_SUB = 16
GLA_CHUNK = 64
GLA_TAU = 16.0
LOG2_E = 1.4426950408889634
V7X_VMEM_LIMIT = 60000 * 1024

TOKEN_ROW_MULTIPLE = 128
MAX_TOKEN_TILE = 1152
FFN_TILE = 256
PROJ_GROUP = 2048
CONV_LANE_TILE = 256
CONV_SAMPLE_LANE_TILE = 512


def _token_tile(t_pad):
    nb = -(-t_pad // MAX_TOKEN_TILE)
    assert t_pad % (16 * nb) == 0, t_pad
    return t_pad // nb


def _proj_tile(t_pad):
    return _token_tile(t_pad) // 2


def _params(*sem):
    return pltpu.CompilerParams(dimension_semantics=sem, vmem_limit_bytes=V7X_VMEM_LIMIT)


def _resident(shape, index_map):
    return pl.BlockSpec(shape, index_map, pipeline_mode=pl.Buffered(1))


def _rms(x, g):
    return x * lax.rsqrt(jnp.mean(x * x, axis=-1, keepdims=True) + EPS) * g


def _silu(x):
    return x * jax.nn.sigmoid(x)


def _dot(a, b):
    return jnp.dot(a, b, preferred_element_type=F32)


def _ffn_kernel(x_ref, g_ref, wg_ref, wu_ref, wd_ref, gf_ref, o_ref, h_ref, *, final_norm):
    f = pl.program_id(1)

    @pl.when(f == 0)
    def _():
        x = x_ref[...]
        h_ref[...] = _rms(x, g_ref[...])
        o_ref[...] = x

    h = h_ref[...]
    a = 0.5 * _silu(_dot(h, wg_ref[...])) * _dot(h, wu_ref[...])
    o_ref[...] += _dot(a, wd_ref[...])

    if final_norm:
        @pl.when(f == pl.num_programs(1) - 1)
        def _():
            o_ref[...] = _rms(o_ref[...], gf_ref[...])


def _ffn(x, g, wg, wu, wd, gf, *, layer, slot, final_norm, windows=None):
    t, d = x.shape
    f = wg.shape[-1]
    tf = FFN_TILE
    if windows is None:
        tm = _token_tile(t)
        n_tiles = t // tm
        x_spec = pl.BlockSpec((tm, d), lambda i, j: (i, 0))
    else:
        tm, n_tiles, row_start = windows
        x_spec = pl.BlockSpec((pl.Element(tm), pl.Element(d)), lambda i, j: (pl.multiple_of(row_start(i), 8), 0))
    return pl.pallas_call(
        functools.partial(_ffn_kernel, final_norm=final_norm),
        grid=(n_tiles, f // tf),
        in_specs=[
            x_spec,
            pl.BlockSpec((None, None, 1, d), lambda i, j: (layer, slot, 0, 0)),
            pl.BlockSpec((None, None, d, tf), lambda i, j: (layer, slot, 0, j)),
            pl.BlockSpec((None, None, d, tf), lambda i, j: (layer, slot, 0, j)),
            pl.BlockSpec((None, None, tf, d), lambda i, j: (layer, slot, j, 0)),
            pl.BlockSpec((1, d), lambda i, j: (0, 0)),
        ],
        out_specs=pl.BlockSpec((tm, d), lambda i, j: (i, 0)),
        out_shape=jax.ShapeDtypeStruct((n_tiles * tm, d), F32),
        scratch_shapes=[pltpu.VMEM((tm, d), F32)],
        compiler_params=_params("parallel", "arbitrary"),
        name="ffn_final" if final_norm else "ffn",
    )(x, g.reshape(g.shape[0], g.shape[1], 1, d), wg, wu, wd, gf.reshape(1, d))


def _row_scale(x):
    return lax.rsqrt(jnp.mean(x * x, axis=-1, keepdims=True) + EPS)


def _pw1_kernel(x_ref, g_ref, wa_ref, wb_ref, ba_ref, bb_ref, u_ref):
    x = x_ref[...]
    xg = x * g_ref[...]
    rs = _row_scale(x)
    a = _dot(xg, wa_ref[...]) * rs + ba_ref[...]
    b = _dot(xg, wb_ref[...]) * rs + bb_ref[...]
    u_ref[...] = a * jax.nn.sigmoid(b)


def _pw1_glu(x, g, w, b):
    t, d = x.shape
    dc = w.shape[1] // 2
    tm, tn = _proj_tile(t), PROJ_GROUP // 2
    nc = dc // tn
    b2 = b.reshape(1, 2 * dc)
    return pl.pallas_call(
        _pw1_kernel,
        grid=(nc, t // tm),
        in_specs=[
            pl.BlockSpec((tm, d), lambda c, i: (i, 0)),
            pl.BlockSpec((1, d), lambda c, i: (0, 0)),
            _resident((d, tn), lambda c, i: (0, c)),
            _resident((d, tn), lambda c, i: (0, c + nc)),
            pl.BlockSpec((1, tn), lambda c, i: (0, c)),
            pl.BlockSpec((1, tn), lambda c, i: (0, c + nc)),
        ],
        out_specs=pl.BlockSpec((tm, tn), lambda c, i: (i, c)),
        out_shape=jax.ShapeDtypeStruct((t, dc), F32),
        compiler_params=_params("arbitrary", "arbitrary"),
        name="conv_pw1_glu",
    )(x, g.reshape(1, d), w, w, b2, b2)


def _conv_window(win, wb_ref, b_ref, rc):
    lead = HALO_PAD - CONV_HALO
    tl = win.shape[1]
    n_tiles = rc // 8 + 1
    acc = jnp.broadcast_to(b_ref[...], (rc, tl))
    for s in range(8):
        part = None
        for a in range((CONV_TAPS + lead + 7) // 8):
            k = 8 * a + s - lead
            if 0 <= k < CONV_TAPS:
                term = (win[8 * a:8 * a + rc + 8].reshape(n_tiles, 8, tl) * wb_ref[k][None]).reshape(rc + 8, tl)
                part = term if part is None else part + term
        acc = acc + (part[:rc] if s == 0 else pltpu.roll(part, rc + 8 - s, axis=0)[:rc])
    return acc


def _dwconv_prompt_kernel(cur_ref, w_ref, b_ref, y_ref, ext_ref, wb_ref, *, rc):
    lp, tl = cur_ref.shape
    ext_ref[0:HALO_PAD, :] = jnp.zeros((HALO_PAD, tl), F32)
    ext_ref[HALO_PAD:HALO_PAD + lp, :] = cur_ref[...]
    ext_ref[HALO_PAD + lp:HALO_PAD + lp + 8, :] = jnp.zeros((8, tl), F32)
    for k in range(CONV_TAPS):
        wb_ref[k] = jnp.broadcast_to(w_ref[k:k + 1, :], (8, tl))

    def body(c, carry):
        r0 = pl.multiple_of(c * rc, 8)
        y_ref[pl.ds(r0, rc), :] = _conv_window(ext_ref[pl.ds(r0, rc + HALO_PAD + 8), :], wb_ref, b_ref, rc)
        return carry

    lax.fori_loop(0, lp // rc, body, 0)


def _dwconv_prompt(u, w, b, *, n_seq, seq_len):
    d = u.shape[1]
    tl = CONV_LANE_TILE
    rc = max(r for r in range(8, 65, 8) if seq_len % r == 0)
    return pl.pallas_call(
        functools.partial(_dwconv_prompt_kernel, rc=rc),
        grid=(n_seq, d // tl),
        in_specs=[
            pl.BlockSpec((seq_len, tl), lambda i, j: (i, j)),
            pl.BlockSpec((CONV_TAPS, tl), lambda i, j: (0, j)),
            pl.BlockSpec((1, tl), lambda i, j: (0, j)),
        ],
        out_specs=pl.BlockSpec((seq_len, tl), lambda i, j: (i, j)),
        out_shape=jax.ShapeDtypeStruct(u.shape, F32),
        scratch_shapes=[pltpu.VMEM((HALO_PAD + seq_len + 8, tl), F32), pltpu.VMEM((CONV_TAPS, 8, tl), F32)],
        compiler_params=_params("parallel", "parallel"),
        name="dwconv_prompt",
    )(u, w, b.reshape(1, d))


def _dwconv_sample_kernel(st_ref, u_ref, w_ref, b_ref, y_hbm_ref, y_ref, ext_ref, *, n_conv_steps):
    del y_hbm_ref
    ls = u_ref.shape[0]
    q = pl.program_id(1)
    t = q % ls

    @pl.when((t == 0) & (q < n_conv_steps))
    def _():
        ext_ref[0:CONV_HALO] = st_ref[...]
        ext_ref[CONV_HALO:CONV_HALO + ls] = u_ref[...]

    @pl.when(q < n_conv_steps)
    def _():
        acc = jnp.broadcast_to(b_ref[...], y_ref.shape)
        for k in range(CONV_TAPS):
            acc = acc + w_ref[k:k + 1, :] * ext_ref[t + k]
        y_ref[...] = acc

    @pl.when(q >= n_conv_steps)
    def _():
        y_ref[...] = jnp.zeros_like(y_ref)


def _dwconv_sample(y_flat, state_t, u_t, w, b, *, row0):
    ls, bs, d = u_t.shape
    tl = CONV_SAMPLE_LANE_TILE
    sb = math.gcd(row0, bs)
    nsb = bs // sb
    n_pad = y_flat.shape[0] - row0 - ls * bs
    assert sb % 8 == 0 and n_pad % sb == 0
    n_conv = ls * nsb

    def seq_block(q):
        return jnp.minimum(q // ls, nsb - 1)

    def out_block(q):
        return row0 // sb + jnp.where(q < n_conv, (q % ls) * nsb + q // ls, q)

    return pl.pallas_call(
        functools.partial(_dwconv_sample_kernel, n_conv_steps=n_conv),
        grid=(d // tl, n_conv + n_pad // sb),
        in_specs=[
            pl.BlockSpec((CONV_HALO, sb, tl), lambda j, q: (0, seq_block(q), j)),
            pl.BlockSpec((ls, sb, tl), lambda j, q: (0, seq_block(q), j)),
            pl.BlockSpec((CONV_TAPS, tl), lambda j, q: (0, j)),
            pl.BlockSpec((1, tl), lambda j, q: (0, j)),
            pl.BlockSpec(memory_space=pl.ANY),
        ],
        out_specs=pl.BlockSpec((sb, tl), lambda j, q: (out_block(q), j)),
        out_shape=jax.ShapeDtypeStruct(y_flat.shape, F32),
        scratch_shapes=[pltpu.VMEM((CONV_HALO + ls, sb, tl), F32)],
        input_output_aliases={4: 0},
        compiler_params=_params("parallel", "arbitrary"),
        name="dwconv_sample",
    )(state_t, u_t, w, b.reshape(1, d), y_flat)


def _ln_pw2_kernel(y_ref, x_ref, lg_ref, lb_ref, w_ref, b_ref, o_ref):
    y = y_ref[...]
    mu = jnp.mean(y, axis=-1, keepdims=True)
    yc = y - mu
    var = jnp.mean(yc * yc, axis=-1, keepdims=True)
    z = _silu(yc * lax.rsqrt(var + EPS) * lg_ref[...] + lb_ref[...])
    o_ref[...] = x_ref[...] + _dot(z, w_ref[...]) + b_ref[...]


def _ln_pw2(y, x, lg, lb, w, b):
    t, d = x.shape
    dc = y.shape[1]
    tm = _proj_tile(t)
    return pl.pallas_call(
        _ln_pw2_kernel,
        grid=(t // tm,),
        in_specs=[
            pl.BlockSpec((tm, dc), lambda i: (i, 0)),
            pl.BlockSpec((tm, d), lambda i: (i, 0)),
            pl.BlockSpec((1, dc), lambda i: (0, 0)),
            pl.BlockSpec((1, dc), lambda i: (0, 0)),
            _resident((dc, d), lambda i: (0, 0)),
            pl.BlockSpec((1, d), lambda i: (0, 0)),
        ],
        out_specs=pl.BlockSpec((tm, d), lambda i: (i, 0)),
        out_shape=jax.ShapeDtypeStruct((t, d), F32),
        compiler_params=_params("parallel"),
        name="conv_ln_pw2",
    )(y, x, lg.reshape(1, dc), lb.reshape(1, dc), w, b.reshape(1, d))


def _log_sigmoid(z):
    return jnp.minimum(z, 0.0) - jnp.log1p(jnp.exp(-jnp.abs(z)))


def _norm_proj_kernel(x_ref, g_ref, *refs):
    *w_refs, p_ref = refs
    x = x_ref[...]
    xg = x * g_ref[...]
    rs = _row_scale(x)
    parts = [_dot(xg, w_ref[...]) * rs for w_ref in w_refs]
    p_ref[...] = parts[0] if len(parts) == 1 else jnp.concatenate(parts, axis=1)


def _norm_proj(x, g, ws, *, name):
    t, d = x.shape
    n = sum(w.shape[1] for w in ws)
    tm = _proj_tile(t)
    return pl.pallas_call(
        _norm_proj_kernel,
        grid=(t // tm,),
        in_specs=[pl.BlockSpec((tm, d), lambda i: (i, 0)), pl.BlockSpec((1, d), lambda i: (0, 0))]
        + [_resident(w.shape, lambda i: (0, 0)) for w in ws],
        out_specs=pl.BlockSpec((tm, n), lambda i: (i, 0)),
        out_shape=jax.ShapeDtypeStruct((t, n), F32),
        compiler_params=_params("parallel"),
        name=name,
    )(x, g.reshape(1, d), *ws)


def _gla_qk_decay_kernel(x_ref, g_ref, wq_ref, wk_ref, wg1_ref, wg2_ref, bg_ref, p_ref, la_ref):
    x = x_ref[...]
    xg = x * g_ref[...]
    rs = _row_scale(x)
    p_ref[...] = jnp.concatenate([_dot(xg, wq_ref[...]) * rs, _dot(xg, wk_ref[...]) * rs], axis=1)
    z = _dot(_dot(xg, wg1_ref[...]) * rs, wg2_ref[...]) + bg_ref[...]
    la_ref[...] = _log_sigmoid(z) * (1.0 / GLA_TAU)


def _gla_qk_decay(x, g, wq, wk, wg1, wg2, bg):
    t, d = x.shape
    r, nk = wg2.shape
    n = wq.shape[1] + wk.shape[1]
    tm = _proj_tile(t)
    return pl.pallas_call(
        _gla_qk_decay_kernel,
        grid=(t // tm,),
        in_specs=[
            pl.BlockSpec((tm, d), lambda i: (i, 0)),
            pl.BlockSpec((1, d), lambda i: (0, 0)),
            _resident(wq.shape, lambda i: (0, 0)),
            _resident(wk.shape, lambda i: (0, 0)),
            _resident((d, r), lambda i: (0, 0)),
            _resident((r, nk), lambda i: (0, 0)),
            pl.BlockSpec((1, nk), lambda i: (0, 0)),
        ],
        out_specs=[pl.BlockSpec((tm, n), lambda i: (i, 0)), pl.BlockSpec((tm, nk), lambda i: (i, 0))],
        out_shape=[jax.ShapeDtypeStruct((t, n), F32), jax.ShapeDtypeStruct((t, nk), F32)],
        compiler_params=_params("parallel"),
        name="gla_proj_qk_decay",
    )(x, g.reshape(1, d), wq, wk, wg1, wg2, bg.reshape(1, nk))


def _bf16_limbs(x):
    hi = x.astype(jnp.bfloat16).astype(F32)
    mid = (x - hi).astype(jnp.bfloat16).astype(F32)
    return hi, mid, x - hi - mid


def _gla_gate(o, r, gn):
    return o * lax.rsqrt(jnp.mean(o * o, axis=-1, keepdims=True) + EPS) * gn * _silu(r)


def _gla_chunk(q, k, v, la, s0):
    c, dk = q.shape
    n_sub = c // GLA_SUB
    neg_inf = -jnp.inf
    row = lax.broadcasted_iota(jnp.int32, (c, c), 0)
    col = lax.broadcasted_iota(jnp.int32, (c, c), 1)
    tri = jnp.where(row >= col, 1.0, 0.0).astype(F32)
    b = _dot(jnp.concatenate([tri, tri, tri], axis=1), jnp.concatenate(_bf16_limbs(la), axis=0)) * LOG2_E

    o = _dot(q * jnp.exp2(b), s0)

    row_c = lax.broadcasted_iota(jnp.int32, (c, 1), 0)
    lane_c = lax.broadcasted_iota(jnp.int32, (GLA_SUB, c), 1)
    t_sub = lax.broadcasted_iota(jnp.int32, (GLA_SUB, 1), 0)
    blocks = []
    for i in range(n_sub):
        lo = GLA_SUB * i
        qi, ki, bi = q[lo:lo + GLA_SUB], k[lo:lo + GLA_SUB], b[lo:lo + GLA_SUB]
        if i == 0:
            sc = jnp.zeros((GLA_SUB, c), F32)
        else:
            b_start = b[lo - 1:lo]
            qt = qi * jnp.exp2(bi - b_start)
            kt = k * jnp.exp2(jnp.where(row_c < lo, b_start - b, neg_inf))
            sc = lax.dot_general(qt, kt, (((1,), (1,)), ((), ())), preferred_element_type=F32)
        for s in range(GLA_SUB):
            colv = jnp.sum(qi * jnp.exp2(bi - bi[s:s + 1]) * ki[s:s + 1], axis=-1, keepdims=True)
            sc = sc + jnp.where(lane_c == lo + s, jnp.where(t_sub >= s, colv, 0.0), 0.0)
        blocks.append(sc)
    scores = blocks[0] if n_sub == 1 else jnp.concatenate(blocks, axis=0)
    o = o + _dot(scores, v)

    b_last = b[c - 1:c]
    kd = k * jnp.exp2(b_last - b)
    upd = lax.dot_general(kd, v, (((0,), (0,)), ((), ())), preferred_element_type=F32)
    s_new = _column_scale(jnp.exp2(b_last), s0) + upd
    return o, s_new


def _column_scale(d_row, s):
    dk, dv = s.shape
    d_col = jnp.transpose(jnp.broadcast_to(d_row, (128, dk)))
    return jnp.concatenate([d_col] * (dv // 128), axis=1) * s


def _gla_prompt_kernel(q_ref, k_ref, v_ref, r_ref, la_ref, gn_ref, og_ref, sn_ref, s_ref):
    rb = q_ref.shape[0]
    hps, dk, dv = s_ref.shape
    q_scale = dk ** -0.5
    j = pl.program_id(2)

    @pl.when(j == 0)
    def _():
        s_ref[...] = jnp.zeros_like(s_ref)

    def run(r0, c):
        rows = pl.ds(r0, c)
        for h in range(hps):
            kc, vc = slice(h * dk, (h + 1) * dk), slice(h * dv, (h + 1) * dv)
            o, s_new = _gla_chunk(q_ref[rows, kc] * q_scale, k_ref[rows, kc], v_ref[rows, vc], la_ref[rows, kc],
                                  s_ref[h])
            s_ref[h] = s_new
            og_ref[rows, vc] = _gla_gate(o, r_ref[rows, vc], gn_ref[...])

    n_full = rb // GLA_CHUNK

    def body(i, carry):
        run(pl.multiple_of(i * GLA_CHUNK, GLA_CHUNK), GLA_CHUNK)
        return carry

    lax.fori_loop(0, n_full, body, 0)
    if rb > n_full * GLA_CHUNK:
        run(n_full * GLA_CHUNK, rb - n_full * GLA_CHUNK)

    @pl.when(j == pl.num_programs(2) - 1)
    def _():
        sn_ref[0] = s_ref[...]


def _gla_prompt(qk, v, r, la, gn, *, n_seq, seq_len):
    h, hps = GLA_HEADS, GLA_HEADS_PER_STEP
    dk = la.shape[1] // h
    dv = gn.shape[0]
    rb = max(r_ for r_ in range(GLA_SUB, 1025, GLA_SUB) if seq_len % r_ == 0)
    nrb, ng = seq_len // rb, h // hps
    assert dv == 2 * dk
    return pl.pallas_call(
        _gla_prompt_kernel,
        grid=(n_seq, ng, nrb),
        in_specs=[
            pl.BlockSpec((rb, hps * dk), lambda b, g, j: (b * nrb + j, g)),
            pl.BlockSpec((rb, hps * dk), lambda b, g, j: (b * nrb + j, ng + g)),
            pl.BlockSpec((rb, hps * dv), lambda b, g, j: (b * nrb + j, g)),
            pl.BlockSpec((rb, hps * dv), lambda b, g, j: (b * nrb + j, g)),
            pl.BlockSpec((rb, hps * dk), lambda b, g, j: (b * nrb + j, g)),
            pl.BlockSpec((1, dv), lambda b, g, j: (0, 0)),
        ],
        out_specs=[
            pl.BlockSpec((rb, hps * dv), lambda b, g, j: (b * nrb + j, g)),
            pl.BlockSpec((1, hps, dk, dv), lambda b, g, j: (b, g, 0, 0)),
        ],
        out_shape=[jax.ShapeDtypeStruct(v.shape, F32),
                   jax.ShapeDtypeStruct((n_seq, h, dk, dv), F32)],
        scratch_shapes=[pltpu.VMEM((hps, dk, dv), F32)],
        compiler_params=_params("parallel", "parallel", "arbitrary"),
        name="gla_prompt",
    )(qk, qk, v, r, la, gn.reshape(1, dv))


def _gla_sample_kernel(p_ref, la_ref, s_ref, gn_ref, og_ref, sn_ref, *, seq_len):
    rows = p_ref.shape[0]
    n_seq = rows // seq_len
    dk = la_ref.shape[1] // GLA_HEADS
    dv = 2 * dk
    q_scale = dk ** -0.5
    neg_inf = -jnp.inf
    t_row = lax.broadcasted_iota(jnp.int32, (rows, 1), 0)
    for hd in range(GLA_HEADS):
        q = p_ref[:, hd * dk:(hd + 1) * dk] * q_scale
        k = p_ref[:, (GLA_HEADS + hd) * dk:(GLA_HEADS + hd + 1) * dk]
        v = p_ref[:, (GLA_HEADS + hd) * dv:(GLA_HEADS + hd + 1) * dv]
        r = p_ref[:, (2 * GLA_HEADS + hd) * dv:(2 * GLA_HEADS + hd + 1) * dv]
        la = la_ref[:, hd * dk:(hd + 1) * dk]

        def in_seq_from(s):
            return (t_row >= s) & (t_row < (s // seq_len + 1) * seq_len)

        b = jnp.zeros_like(la)
        for s in range(rows):
            b = b + jnp.where(in_seq_from(s), la[s:s + 1], 0.0)

        qe = q * jnp.exp(b)
        o = jnp.zeros((rows, dv), F32)
        for g in range(n_seq):
            mine = (t_row >= g * seq_len) & (t_row < (g + 1) * seq_len)
            o = jnp.where(mine, _dot(qe, s_ref[g, hd]), o)
        for s in range(rows):
            dec = jnp.exp(jnp.where(in_seq_from(s), b - b[s:s + 1], neg_inf))
            colv = jnp.sum(q * dec * k[s:s + 1], axis=-1, keepdims=True)
            o = o + colv * v[s:s + 1]
        og_ref[:, hd * dv:(hd + 1) * dv] = _gla_gate(o, r, gn_ref[...])

        for g in range(n_seq):
            last = (g + 1) * seq_len - 1
            b_last = b[last:last + 1]
            mine = (t_row >= g * seq_len) & (t_row < (g + 1) * seq_len)
            kd = k * jnp.exp(jnp.where(mine, b_last - b, neg_inf))
            upd = lax.dot_general(kd, v, (((0,), (0,)), ((), ())), preferred_element_type=F32)
            sn_ref[g, hd] = _column_scale(jnp.exp(b_last), s_ref[g, hd]) + upd


def _gla_sample(p_s, la_s, state, gn, *, seq_len):
    rows = 8
    n, h, dk, dv = state.shape
    t = p_s.shape[0]
    per = rows // seq_len
    return pl.pallas_call(
        functools.partial(_gla_sample_kernel, seq_len=seq_len),
        grid=(t // rows,),
        in_specs=[
            pl.BlockSpec((rows, p_s.shape[1]), lambda i: (i, 0)),
            pl.BlockSpec((rows, la_s.shape[1]), lambda i: (i, 0)),
            pl.BlockSpec((per, h, dk, dv), lambda i: (i, 0, 0, 0)),
            pl.BlockSpec((1, dv), lambda i: (0, 0)),
        ],
        out_specs=[
            pl.BlockSpec((rows, h * dv), lambda i: (i, 0)),
            pl.BlockSpec((per, h, dk, dv), lambda i: (i, 0, 0, 0)),
        ],
        out_shape=[jax.ShapeDtypeStruct((t, h * dv), F32), jax.ShapeDtypeStruct(state.shape, F32)],
        compiler_params=_params("parallel"),
        name="gla_sample",
    )(p_s, la_s, state, gn.reshape(1, dv))


def _out_proj_kernel(a_ref, x_ref, w_ref, o_ref):
    o_ref[...] = x_ref[...] + _dot(a_ref[...], w_ref[...])


def _out_proj(a, x, w):
    t, d = x.shape
    kdim = a.shape[1]
    tm = _proj_tile(t)
    return pl.pallas_call(
        _out_proj_kernel,
        grid=(t // tm,),
        in_specs=[
            pl.BlockSpec((tm, kdim), lambda i: (i, 0)),
            pl.BlockSpec((tm, d), lambda i: (i, 0)),
            _resident((kdim, d), lambda i: (0, 0)),
        ],
        out_specs=pl.BlockSpec((tm, d), lambda i: (i, 0)),
        out_shape=jax.ShapeDtypeStruct((t, d), F32),
        compiler_params=_params("parallel"),
        name="gla_out_proj",
    )(a, x, w)


def kernel(x_prompt, x_sample, state_conv, state_gla, meta_tokens, norm_ffn, w_ffn_gate, w_ffn_up, w_ffn_down,
           norm_mix, conv_w_pw1, conv_b_pw1, conv_w_dw, conv_b_dw, conv_ln_g, conv_ln_b, conv_w_pw2, conv_b_pw2,
           gla_w_q, gla_w_k, gla_w_v, gla_w_g1, gla_w_g2, gla_b_g, gla_w_r, gla_gn_g, gla_w_o, norm_final):
    bp, sp, d = x_prompt.shape
    bs, ls, _ = x_sample.shape
    lp = N_META_ROWS + sp
    tp, ts = bp * lp, bs * ls
    n_pad = -(tp + ts) % TOKEN_ROW_MULTIPLE
    assert 8 % ls == 0 and ts % 8 == 0

    def with_pad(a):
        return jnp.concatenate([a, jnp.zeros((n_pad, a.shape[1]), F32)], axis=0) if n_pad else a

    x = jnp.concatenate(
        [piece for b in range(bp) for piece in (meta_tokens, x_prompt[b])]
        + [with_pad(jnp.transpose(x_sample, (1, 0, 2)).reshape(ts, d))], axis=0)

    def ffn(x, i, j, final_norm=False):
        return _ffn(x, norm_ffn, w_ffn_gate, w_ffn_up, w_ffn_down, norm_final,
                    layer=i, slot=j, final_norm=final_norm)

    def prompt_rows(a, start):
        return jnp.stack([a[b * lp + start:(b + 1) * lp] for b in range(bp)])

    def sample_rows(a):
        return a[tp:tp + ts].reshape(ls, bs, -1)

    x = ffn(x, 0, 0)
    u = _pw1_glu(x, norm_mix[0], conv_w_pw1[0], conv_b_pw1[0])
    u_s = sample_rows(u)
    y = _dwconv_prompt(u, conv_w_dw[0], conv_b_dw[0], n_seq=bp, seq_len=lp)
    y = _dwconv_sample(y, jnp.transpose(state_conv[0], (1, 0, 2)), u_s, conv_w_dw[0], conv_b_dw[0], row0=tp)
    x = _ln_pw2(y, x, conv_ln_g[0], conv_ln_b[0], conv_w_pw2[0], conv_b_pw2[0])
    x = ffn(x, 0, 1)
    new_conv_prompt = prompt_rows(u, lp - CONV_HALO)[None]
    new_conv_sample = jnp.concatenate([state_conv[0], jnp.transpose(u_s, (1, 0, 2))], axis=1)[None, :, ls:]

    x = ffn(x, 1, 0)
    p_qk, la = _gla_qk_decay(x, norm_mix[1], gla_w_q[0], gla_w_k[0], gla_w_g1[0], gla_w_g2[0], gla_b_g[0])
    p_v = _norm_proj(x, norm_mix[1], [gla_w_v[0]], name="gla_proj_v")
    p_r = _norm_proj(x, norm_mix[1], [gla_w_r[0]], name="gla_proj_r")
    og, new_gla_prompt = _gla_prompt(p_qk, p_v, p_r, la, gla_gn_g[0], n_seq=bp, seq_len=lp)

    def seq_major(a):
        return jnp.transpose(sample_rows(a), (1, 0, 2)).reshape(ts, -1)

    p_s = jnp.concatenate([seq_major(p_qk), seq_major(p_v), seq_major(p_r)], axis=1)
    og_s, new_gla_sample = _gla_sample(p_s, seq_major(la), state_gla[0], gla_gn_g[0], seq_len=ls)
    og_s = jnp.transpose(og_s.reshape(bs, ls, -1), (1, 0, 2)).reshape(ts, -1)
    og = lax.dynamic_update_slice(og, with_pad(og_s), (tp, 0))
    x = _out_proj(og, x, gla_w_o[0])

    tm_p = max(r for r in range(8, MAX_TOKEN_TILE + 1, 8) if sp % r == 0)
    per_seq = sp // tm_p
    y_prompt = _ffn(x, norm_ffn, w_ffn_gate, w_ffn_up, w_ffn_down, norm_final, layer=1, slot=1, final_norm=True,
                    windows=(tm_p, bp * per_seq, lambda i: (i // per_seq) * lp + N_META_ROWS + (i % per_seq) * tm_p))
    y_sample_t = _ffn(x, norm_ffn, w_ffn_gate, w_ffn_up, w_ffn_down, norm_final, layer=1, slot=1, final_norm=True,
                      windows=(ts, 1, lambda i: tp + 0 * i))
    y_prompt = y_prompt.reshape(bp, sp, d)
    y_sample = jnp.transpose(y_sample_t.reshape(ls, bs, d), (1, 0, 2))
    return (y_prompt, y_sample, new_conv_prompt, new_gla_prompt[None], new_conv_sample, new_gla_sample[None])
```

```python
import functools
import math

import jax
import jax.numpy as jnp
from jax import lax
from jax.experimental import pallas as pl
from jax.experimental.pallas import tpu as pltpu

F32 = jnp.float32
EPS = 1e-6
N_META_ROWS = 16
CONV_TAPS = 31
CONV_HALO = CONV_TAPS - 1
HALO_PAD = 32
GLA_HEADS = 4
GLA_HEADS_PER_STEP = 4
GLA_SUB = 16
GLA_CHUNK = 128
GLA_TAU = 16.0
LOG2_E = 1.4426950408889634
V7X_VMEM_LIMIT = 60000 * 1024

TOKEN_ROW_MULTIPLE = 128
MAX_TOKEN_TILE = 1152
FFN_TILE = 256
PROJ_GROUP = 2048
CONV_LANE_TILE = 256
CONV_SAMPLE_LANE_TILE = 1024


def _token_tile(t_pad):
    nb = -(-t_pad // MAX_TOKEN_TILE)
    assert t_pad % (16 * nb) == 0, t_pad
    return t_pad // nb


def _proj_tile(t_pad):
    return _token_tile(t_pad) // 2


def _params(*sem):
    return pltpu.CompilerParams(dimension_semantics=sem, vmem_limit_bytes=V7X_VMEM_LIMIT)


def _resident(shape, index_map):
    return pl.BlockSpec(shape, index_map, pipeline_mode=pl.Buffered(1))


def _rms(x, g):
    return x * lax.rsqrt(jnp.mean(x * x, axis=-1, keepdims=True) + EPS) * g


def _silu(x):
    return x * jax.nn.sigmoid(x)


def _dot(a, b):
    return jnp.dot(a, b, preferred_element_type=F32)


def _ffn_kernel(x_ref, g_ref, wg_ref, wu_ref, wd_ref, gf_ref, o_ref, h_ref, *, final_norm):
    f = pl.program_id(1)

    @pl.when(f == 0)
    def _():
        x = x_ref[...]
        h_ref[...] = _rms(x, g_ref[...])
        o_ref[...] = x

    h = h_ref[...]
    a = 0.5 * _silu(_dot(h, wg_ref[...])) * _dot(h, wu_ref[...])
    o_ref[...] += _dot(a, wd_ref[...])

    if final_norm:
        @pl.when(f == pl.num_programs(1) - 1)
        def _():
            o_ref[...] = _rms(o_ref[...], gf_ref[...])


def _ffn(x, g, wg, wu, wd, gf, *, layer, slot, final_norm, windows=None):
    t, d = x.shape
    f = wg.shape[-1]
    tf = FFN_TILE
    if windows is None:
        tm = _token_tile(t)
        n_tiles = t // tm
        x_spec = pl.BlockSpec((tm, d), lambda i, j: (i, 0))
    else:
        tm, n_tiles, row_start = windows
        x_spec = pl.BlockSpec((pl.Element(tm), pl.Element(d)), lambda i, j: (pl.multiple_of(row_start(i), 8), 0))
    return pl.pallas_call(
        functools.partial(_ffn_kernel, final_norm=final_norm),
        grid=(n_tiles, f // tf),
        in_specs=[
            x_spec,
            pl.BlockSpec((None, None, 1, d), lambda i, j: (layer, slot, 0, 0)),
            pl.BlockSpec((None, None, d, tf), lambda i, j: (layer, slot, 0, j)),
            pl.BlockSpec((None, None, d, tf), lambda i, j: (layer, slot, 0, j)),
            pl.BlockSpec((None, None, tf, d), lambda i, j: (layer, slot, j, 0)),
            pl.BlockSpec((1, d), lambda i, j: (0, 0)),
        ],
        out_specs=pl.BlockSpec((tm, d), lambda i, j: (i, 0)),
        out_shape=jax.ShapeDtypeStruct((n_tiles * tm, d), F32),
        scratch_shapes=[pltpu.VMEM((tm, d), F32)],
        compiler_params=_params("parallel", "arbitrary"),
        name="ffn_final" if final_norm else "ffn",
    )(x, g.reshape(g.shape[0], g.shape[1], 1, d), wg, wu, wd, gf.reshape(1, d))


def _row_scale(x):
    return lax.rsqrt(jnp.mean(x * x, axis=-1, keepdims=True) + EPS)


def _pw1_kernel(x_ref, g_ref, wa_ref, wb_ref, ba_ref, bb_ref, u_ref):
    x = x_ref[...]
    xg = x * g_ref[...]
    rs = _row_scale(x)
    a = _dot(xg, wa_ref[...]) * rs + ba_ref[...]
    b = _dot(xg, wb_ref[...]) * rs + bb_ref[...]
    u_ref[...] = a * jax.nn.sigmoid(b)


def _pw1_glu(x, g, w, b):
    t, d = x.shape
    dc = w.shape[1] // 2
    tm, tn = _proj_tile(t), PROJ_GROUP // 2
    nc = dc // tn
    b2 = b.reshape(1, 2 * dc)
    return pl.pallas_call(
        _pw1_kernel,
        grid=(nc, t // tm),
        in_specs=[
            pl.BlockSpec((tm, d), lambda c, i: (i, 0)),
            pl.BlockSpec((1, d), lambda c, i: (0, 0)),
            _resident((d, tn), lambda c, i: (0, c)),
            _resident((d, tn), lambda c, i: (0, c + nc)),
            pl.BlockSpec((1, tn), lambda c, i: (0, c)),
            pl.BlockSpec((1, tn), lambda c, i: (0, c + nc)),
        ],
        out_specs=pl.BlockSpec((tm, tn), lambda c, i: (i, c)),
        out_shape=jax.ShapeDtypeStruct((t, dc), F32),
        compiler_params=_params("arbitrary", "arbitrary"),
        name="conv_pw1_glu",
    )(x, g.reshape(1, d), w, w, b2, b2)


def _conv_window(win, wb_ref, b_ref, rc):
    lead = HALO_PAD - CONV_HALO
    tl = win.shape[1]
    n_tiles = rc // 8 + 1
    acc = jnp.broadcast_to(b_ref[...], (rc, tl))
    for s in range(8):
        part = None
        for a in range((CONV_TAPS + lead + 7) // 8):
            k = 8 * a + s - lead
            if 0 <= k < CONV_TAPS:
                term = (win[8 * a:8 * a + rc + 8].reshape(n_tiles, 8, tl) * wb_ref[k][None]).reshape(rc + 8, tl)
                part = term if part is None else part + term
        acc = acc + (part[:rc] if s == 0 else pltpu.roll(part, rc + 8 - s, axis=0)[:rc])
    return acc


def _dwconv_prompt_kernel(cur_ref, w_ref, b_ref, y_ref, ext_ref, wb_ref, *, rc):
    lp, tl = cur_ref.shape
    ext_ref[0:HALO_PAD, :] = jnp.zeros((HALO_PAD, tl), F32)
    ext_ref[HALO_PAD:HALO_PAD + lp, :] = cur_ref[...]
    ext_ref[HALO_PAD + lp:HALO_PAD + lp + 8, :] = jnp.zeros((8, tl), F32)
    for k in range(CONV_TAPS):
        wb_ref[k] = jnp.broadcast_to(w_ref[k:k + 1, :], (8, tl))

    def body(c, carry):
        r0 = pl.multiple_of(c * rc, 8)
        y_ref[pl.ds(r0, rc), :] = _conv_window(ext_ref[pl.ds(r0, rc + HALO_PAD + 8), :], wb_ref, b_ref, rc)
        return carry

    lax.fori_loop(0, lp // rc, body, 0)


def _dwconv_prompt(u, w, b, *, n_seq, seq_len):
    d = u.shape[1]
    tl = CONV_LANE_TILE
    rc = max(r for r in range(8, 345, 8) if seq_len % r == 0)
    return pl.pallas_call(
        functools.partial(_dwconv_prompt_kernel, rc=rc),
        grid=(n_seq, d // tl),
        in_specs=[
            pl.BlockSpec((seq_len, tl), lambda i, j: (i, j)),
            pl.BlockSpec((CONV_TAPS, tl), lambda i, j: (0, j)),
            pl.BlockSpec((1, tl), lambda i, j: (0, j)),
        ],
        out_specs=pl.BlockSpec((seq_len, tl), lambda i, j: (i, j)),
        out_shape=jax.ShapeDtypeStruct(u.shape, F32),
        scratch_shapes=[pltpu.VMEM((HALO_PAD + seq_len + 8, tl), F32), pltpu.VMEM((CONV_TAPS, 8, tl), F32)],
        compiler_params=_params("parallel", "parallel"),
        name="dwconv_prompt",
    )(u, w, b.reshape(1, d))


def _dwconv_sample_kernel(st_ref, u_ref, w_ref, b_ref, y_hbm_ref, y_ref, ext_ref, *, n_conv_steps):
    del y_hbm_ref
    ls = u_ref.shape[0]
    q = pl.program_id(1)
    t = q % ls

    @pl.when((t == 0) & (q < n_conv_steps))
    def _():
        ext_ref[0:CONV_HALO] = st_ref[...]
        ext_ref[CONV_HALO:CONV_HALO + ls] = u_ref[...]

    @pl.when(q < n_conv_steps)
    def _():
        acc = jnp.broadcast_to(b_ref[...], y_ref.shape)
        for k in range(CONV_TAPS):
            acc = acc + w_ref[k:k + 1, :] * ext_ref[t + k]
        y_ref[...] = acc

    @pl.when(q >= n_conv_steps)
    def _():
        y_ref[...] = jnp.zeros_like(y_ref)


def _dwconv_sample(y_flat, state_t, u_t, w, b, *, row0):
    ls, bs, d = u_t.shape
    tl = CONV_SAMPLE_LANE_TILE
    sb = math.gcd(row0, bs)
    nsb = bs // sb
    n_pad = y_flat.shape[0] - row0 - ls * bs
    assert sb % 8 == 0 and n_pad % sb == 0
    n_conv = ls * nsb

    def seq_block(q):
        return jnp.minimum(q // ls, nsb - 1)

    def out_block(q):
        return row0 // sb + jnp.where(q < n_conv, (q % ls) * nsb + q // ls, q)

    return pl.pallas_call(
        functools.partial(_dwconv_sample_kernel, n_conv_steps=n_conv),
        grid=(d // tl, n_conv + n_pad // sb),
        in_specs=[
            pl.BlockSpec((CONV_HALO, sb, tl), lambda j, q: (0, seq_block(q), j)),
            pl.BlockSpec((ls, sb, tl), lambda j, q: (0, seq_block(q), j)),
            pl.BlockSpec((CONV_TAPS, tl), lambda j, q: (0, j)),
            pl.BlockSpec((1, tl), lambda j, q: (0, j)),
            pl.BlockSpec(memory_space=pl.ANY),
        ],
        out_specs=pl.BlockSpec((sb, tl), lambda j, q: (out_block(q), j)),
        out_shape=jax.ShapeDtypeStruct(y_flat.shape, F32),
        scratch_shapes=[pltpu.VMEM((CONV_HALO + ls, sb, tl), F32)],
        input_output_aliases={4: 0},
        compiler_params=_params("parallel", "arbitrary"),
        name="dwconv_sample",
    )(state_t, u_t, w, b.reshape(1, d), y_flat)


def _ln_pw2_kernel(y_ref, x_ref, lg_ref, lb_ref, w_ref, b_ref, o_ref):
    y = y_ref[...]
    mu = jnp.mean(y, axis=-1, keepdims=True)
    yc = y - mu
    var = jnp.mean(yc * yc, axis=-1, keepdims=True)
    z = _silu(yc * lax.rsqrt(var + EPS) * lg_ref[...] + lb_ref[...])
    o_ref[...] = x_ref[...] + _dot(z, w_ref[...]) + b_ref[...]


def _ln_pw2(y, x, lg, lb, w, b):
    t, d = x.shape
    dc = y.shape[1]
    tm = _proj_tile(t)
    return pl.pallas_call(
        _ln_pw2_kernel,
        grid=(t // tm,),
        in_specs=[
            pl.BlockSpec((tm, dc), lambda i: (i, 0)),
            pl.BlockSpec((tm, d), lambda i: (i, 0)),
            pl.BlockSpec((1, dc), lambda i: (0, 0)),
            pl.BlockSpec((1, dc), lambda i: (0, 0)),
            _resident((dc, d), lambda i: (0, 0)),
            pl.BlockSpec((1, d), lambda i: (0, 0)),
        ],
        out_specs=pl.BlockSpec((tm, d), lambda i: (i, 0)),
        out_shape=jax.ShapeDtypeStruct((t, d), F32),
        compiler_params=_params("parallel"),
        name="conv_ln_pw2",
    )(y, x, lg.reshape(1, dc), lb.reshape(1, dc), w, b.reshape(1, d))


def _log_sigmoid(z):
    return jnp.minimum(z, 0.0) - jnp.log1p(jnp.exp(-jnp.abs(z)))


def _norm_proj_kernel(x_ref, g_ref, *refs):
    *w_refs, p_ref = refs
    x = x_ref[...]
    xg = x * g_ref[...]
    rs = _row_scale(x)
    parts = [_dot(xg, w_ref[...]) * rs for w_ref in w_refs]
    p_ref[...] = parts[0] if len(parts) == 1 else jnp.concatenate(parts, axis=1)


def _norm_proj(x, g, ws, *, name):
    t, d = x.shape
    n = sum(w.shape[1] for w in ws)
    tm = _proj_tile(t)
    return pl.pallas_call(
        _norm_proj_kernel,
        grid=(t // tm,),
        in_specs=[pl.BlockSpec((tm, d), lambda i: (i, 0)), pl.BlockSpec((1, d), lambda i: (0, 0))]
        + [_resident(w.shape, lambda i: (0, 0)) for w in ws],
        out_specs=pl.BlockSpec((tm, n), lambda i: (i, 0)),
        out_shape=jax.ShapeDtypeStruct((t, n), F32),
        compiler_params=_params("parallel"),
        name=name,
    )(x, g.reshape(1, d), *ws)


def _gla_qk_decay_kernel(x_ref, g_ref, wq_ref, wk_ref, wg1_ref, wg2_ref, bg_ref, p_ref, la_ref):
    x = x_ref[...]
    xg = x * g_ref[...]
    rs = _row_scale(x)
    p_ref[...] = jnp.concatenate([_dot(xg, wq_ref[...]) * rs, _dot(xg, wk_ref[...]) * rs], axis=1)
    z = _dot(_dot(xg, wg1_ref[...]) * rs, wg2_ref[...]) + bg_ref[...]
    la_ref[...] = _log_sigmoid(z) * (1.0 / GLA_TAU)


def _gla_qk_decay(x, g, wq, wk, wg1, wg2, bg):
    t, d = x.shape
    r, nk = wg2.shape
    n = wq.shape[1] + wk.shape[1]
    tm = _proj_tile(t)
    return pl.pallas_call(
        _gla_qk_decay_kernel,
        grid=(t // tm,),
        in_specs=[
            pl.BlockSpec((tm, d), lambda i: (i, 0)),
            pl.BlockSpec((1, d), lambda i: (0, 0)),
            _resident(wq.shape, lambda i: (0, 0)),
            _resident(wk.shape, lambda i: (0, 0)),
            _resident((d, r), lambda i: (0, 0)),
            _resident((r, nk), lambda i: (0, 0)),
            pl.BlockSpec((1, nk), lambda i: (0, 0)),
        ],
        out_specs=[pl.BlockSpec((tm, n), lambda i: (i, 0)), pl.BlockSpec((tm, nk), lambda i: (i, 0))],
        out_shape=[jax.ShapeDtypeStruct((t, n), F32), jax.ShapeDtypeStruct((t, nk), F32)],
        compiler_params=_params("parallel"),
        name="gla_proj_qk_decay",
    )(x, g.reshape(1, d), wq, wk, wg1, wg2, bg.reshape(1, nk))


def _bf16_limbs(x):
    hi = x.astype(jnp.bfloat16).astype(F32)
    mid = (x - hi).astype(jnp.bfloat16).astype(F32)
    return hi, mid, x - hi - mid


def _gla_gate(o, r, gn):
    return o * lax.rsqrt(jnp.mean(o * o, axis=-1, keepdims=True) + EPS) * gn * _silu(r)


def _gla_chunk(q, k, v, la, s0):
    c, dk = q.shape
    n_sub = c // GLA_SUB
    neg_inf = -jnp.inf
    row = lax.broadcasted_iota(jnp.int32, (c, c), 0)
    col = lax.broadcasted_iota(jnp.int32, (c, c), 1)
    tri = jnp.where(row >= col, 1.0, 0.0).astype(F32)
    b = _dot(jnp.concatenate([tri, tri, tri], axis=1), jnp.concatenate(_bf16_limbs(la), axis=0)) * LOG2_E

    o = _dot(q * jnp.exp2(b), s0)

    row_c = lax.broadcasted_iota(jnp.int32, (c, 1), 0)
    lane_c = lax.broadcasted_iota(jnp.int32, (GLA_SUB, c), 1)
    t_sub = lax.broadcasted_iota(jnp.int32, (GLA_SUB, 1), 0)
    blocks = []
    for i in range(n_sub):
        lo = GLA_SUB * i
        qi, ki, bi = q[lo:lo + GLA_SUB], k[lo:lo + GLA_SUB], b[lo:lo + GLA_SUB]
        if i == 0:
            sc = jnp.zeros((GLA_SUB, c), F32)
        else:
            b_start = b[lo - 1:lo]
            qt = qi * jnp.exp2(bi - b_start)
            kt = k * jnp.exp2(jnp.where(row_c < lo, b_start - b, neg_inf))
            sc = lax.dot_general(qt, kt, (((1,), (1,)), ((), ())), preferred_element_type=F32)
        for s in range(GLA_SUB):
            colv = jnp.sum(qi * jnp.exp2(bi - bi[s:s + 1]) * ki[s:s + 1], axis=-1, keepdims=True)
            sc = sc + jnp.where(lane_c == lo + s, jnp.where(t_sub >= s, colv, 0.0), 0.0)
        blocks.append(sc)
    scores = blocks[0] if n_sub == 1 else jnp.concatenate(blocks, axis=0)
    o = o + _dot(scores, v)

    b_last = b[c - 1:c]
    kd = k * jnp.exp2(b_last - b)
    upd = lax.dot_general(kd, v, (((0,), (0,)), ((), ())), preferred_element_type=F32)
    s_new = _column_scale(jnp.exp2(b_last), s0) + upd
    return o, s_new


def _column_scale(d_row, s):
    dk, dv = s.shape
    d_col = jnp.transpose(jnp.broadcast_to(d_row, (128, dk)))
    return jnp.concatenate([d_col] * (dv // 128), axis=1) * s


def _gla_prompt_kernel(q_ref, k_ref, v_ref, r_ref, la_ref, gn_ref, og_ref, sn_ref, s_ref):
    rb = q_ref.shape[0]
    hps, dk, dv = s_ref.shape
    q_scale = dk ** -0.5
    j = pl.program_id(2)

    @pl.when(j == 0)
    def _():
        s_ref[...] = jnp.zeros_like(s_ref)

    def run(r0, c):
        rows = pl.ds(r0, c)
        for h in range(hps):
            kc, vc = slice(h * dk, (h + 1) * dk), slice(h * dv, (h + 1) * dv)
            o, s_new = _gla_chunk(q_ref[rows, kc] * q_scale, k_ref[rows, kc], v_ref[rows, vc], la_ref[rows, kc],
                                  s_ref[h])
            s_ref[h] = s_new
            og_ref[rows, vc] = _gla_gate(o, r_ref[rows, vc], gn_ref[...])

    n_full = rb // GLA_CHUNK

    def body(i, carry):
        run(pl.multiple_of(i * GLA_CHUNK, GLA_CHUNK), GLA_CHUNK)
        return carry

    lax.fori_loop(0, n_full, body, 0)
    if rb > n_full * GLA_CHUNK:
        run(n_full * GLA_CHUNK, rb - n_full * GLA_CHUNK)

    @pl.when(j == pl.num_programs(2) - 1)
    def _():
        sn_ref[0] = s_ref[...]


def _gla_prompt(qk, v, r, la, gn, *, n_seq, seq_len):
    h, hps = GLA_HEADS, GLA_HEADS_PER_STEP
    dk = la.shape[1] // h
    dv = gn.shape[0]
    rb = max(r_ for r_ in range(GLA_SUB, 1025, GLA_SUB) if seq_len % r_ == 0)
    nrb, ng = seq_len // rb, h // hps
    assert dv == 2 * dk
    return pl.pallas_call(
        _gla_prompt_kernel,
        grid=(n_seq, ng, nrb),
        in_specs=[
            pl.BlockSpec((rb, hps * dk), lambda b, g, j: (b * nrb + j, g)),
            pl.BlockSpec((rb, hps * dk), lambda b, g, j: (b * nrb + j, ng + g)),
            pl.BlockSpec((rb, hps * dv), lambda b, g, j: (b * nrb + j, g)),
            pl.BlockSpec((rb, hps * dv), lambda b, g, j: (b * nrb + j, g)),
            pl.BlockSpec((rb, hps * dk), lambda b, g, j: (b * nrb + j, g)),
            pl.BlockSpec((1, dv), lambda b, g, j: (0, 0)),
        ],
        out_specs=[
            pl.BlockSpec((rb, hps * dv), lambda b, g, j: (b * nrb + j, g)),
            pl.BlockSpec((1, hps, dk, dv), lambda b, g, j: (b, g, 0, 0)),
        ],
        out_shape=[jax.ShapeDtypeStruct(v.shape, F32),
                   jax.ShapeDtypeStruct((n_seq, h, dk, dv), F32)],
        scratch_shapes=[pltpu.VMEM((hps, dk, dv), F32)],
        compiler_params=_params("parallel", "parallel", "arbitrary"),
        name="gla_prompt",
    )(qk, qk, v, r, la, gn.reshape(1, dv))


def _gla_sample_kernel(p_ref, la_ref, s_ref, gn_ref, og_ref, sn_ref, *, seq_len):
    rows = p_ref.shape[0]
    n_seq = rows // seq_len
    dk = la_ref.shape[1] // GLA_HEADS
    dv = 2 * dk
    q_scale = dk ** -0.5
    neg_inf = -jnp.inf
    t_row = lax.broadcasted_iota(jnp.int32, (rows, 1), 0)
    for hd in range(GLA_HEADS):
        q = p_ref[:, hd * dk:(hd + 1) * dk] * q_scale
        k = p_ref[:, (GLA_HEADS + hd) * dk:(GLA_HEADS + hd + 1) * dk]
        v = p_ref[:, (GLA_HEADS + hd) * dv:(GLA_HEADS + hd + 1) * dv]
        r = p_ref[:, (2 * GLA_HEADS + hd) * dv:(2 * GLA_HEADS + hd + 1) * dv]
        la = la_ref[:, hd * dk:(hd + 1) * dk]

        def in_seq_from(s):
            return (t_row >= s) & (t_row < (s // seq_len + 1) * seq_len)

        b = jnp.zeros_like(la)
        for s in range(rows):
            b = b + jnp.where(in_seq_from(s), la[s:s + 1], 0.0)

        qe = q * jnp.exp(b)
        o = jnp.zeros((rows, dv), F32)
        for g in range(n_seq):
            mine = (t_row >= g * seq_len) & (t_row < (g + 1) * seq_len)
            o = jnp.where(mine, _dot(qe, s_ref[g, hd]), o)
        for s in range(rows):
            dec = jnp.exp(jnp.where(in_seq_from(s), b - b[s:s + 1], neg_inf))
            colv = jnp.sum(q * dec * k[s:s + 1], axis=-1, keepdims=True)
            o = o + colv * v[s:s + 1]
        og_ref[:, hd * dv:(hd + 1) * dv] = _gla_gate(o, r, gn_ref[...])

        for g in range(n_seq):
            last = (g + 1) * seq_len - 1
            b_last = b[last:last + 1]
            mine = (t_row >= g * seq_len) & (t_row < (g + 1) * seq_len)
            kd = k * jnp.exp(jnp.where(mine, b_last - b, neg_inf))
            upd = lax.dot_general(kd, v, (((0,), (0,)), ((), ())), preferred_element_type=F32)
            sn_ref[g, hd] = _column_scale(jnp.exp(b_last), s_ref[g, hd]) + upd


def _gla_sample(p_s, la_s, state, gn, *, seq_len):
    rows = 8
    n, h, dk, dv = state.shape
    t = p_s.shape[0]
    per = rows // seq_len
    return pl.pallas_call(
        functools.partial(_gla_sample_kernel, seq_len=seq_len),
        grid=(t // rows,),
        in_specs=[
            pl.BlockSpec((rows, p_s.shape[1]), lambda i: (i, 0)),
            pl.BlockSpec((rows, la_s.shape[1]), lambda i: (i, 0)),
            pl.BlockSpec((per, h, dk, dv), lambda i: (i, 0, 0, 0)),
            pl.BlockSpec((1, dv), lambda i: (0, 0)),
        ],
        out_specs=[
            pl.BlockSpec((rows, h * dv), lambda i: (i, 0)),
            pl.BlockSpec((per, h, dk, dv), lambda i: (i, 0, 0, 0)),
        ],
        out_shape=[jax.ShapeDtypeStruct((t, h * dv), F32), jax.ShapeDtypeStruct(state.shape, F32)],
        compiler_params=_params("parallel"),
        name="gla_sample",
    )(p_s, la_s, state, gn.reshape(1, dv))


def _out_proj_kernel(a_ref, x_ref, w_ref, o_ref):
    o_ref[...] = x_ref[...] + _dot(a_ref[...], w_ref[...])


def _out_proj(a, x, w):
    t, d = x.shape
    kdim = a.shape[1]
    tm = _proj_tile(t)
    return pl.pallas_call(
        _out_proj_kernel,
        grid=(t // tm,),
        in_specs=[
            pl.BlockSpec((tm, kdim), lambda i: (i, 0)),
            pl.BlockSpec((tm, d), lambda i: (i, 0)),
            _resident((kdim, d), lambda i: (0, 0)),
        ],
        out_specs=pl.BlockSpec((tm, d), lambda i: (i, 0)),
        out_shape=jax.ShapeDtypeStruct((t, d), F32),
        compiler_params=_params("parallel"),
        name="gla_out_proj",
    )(a, x, w)


def kernel(x_prompt, x_sample, state_conv, state_gla, meta_tokens, norm_ffn, w_ffn_gate, w_ffn_up, w_ffn_down,
           norm_mix, conv_w_pw1, conv_b_pw1, conv_w_dw, conv_b_dw, conv_ln_g, conv_ln_b, conv_w_pw2, conv_b_pw2,
           gla_w_q, gla_w_k, gla_w_v, gla_w_g1, gla_w_g2, gla_b_g, gla_w_r, gla_gn_g, gla_w_o, norm_final):
    bp, sp, d = x_prompt.shape
    bs, ls, _ = x_sample.shape
    lp = N_META_ROWS + sp
    tp, ts = bp * lp, bs * ls
    n_pad = -(tp + ts) % TOKEN_ROW_MULTIPLE
    assert 8 % ls == 0 and ts % 8 == 0

    def with_pad(a):
        return jnp.concatenate([a, jnp.zeros((n_pad, a.shape[1]), F32)], axis=0) if n_pad else a

    x = jnp.concatenate(
        [piece for b in range(bp) for piece in (meta_tokens, x_prompt[b])]
        + [with_pad(jnp.transpose(x_sample, (1, 0, 2)).reshape(ts, d))], axis=0)

    def ffn(x, i, j, final_norm=False):
        return _ffn(x, norm_ffn, w_ffn_gate, w_ffn_up, w_ffn_down, norm_final,
                    layer=i, slot=j, final_norm=final_norm)

    def prompt_rows(a, start):
        return jnp.stack([a[b * lp + start:(b + 1) * lp] for b in range(bp)])

    def sample_rows(a):
        return a[tp:tp + ts].reshape(ls, bs, -1)

    x = ffn(x, 0, 0)
    u = _pw1_glu(x, norm_mix[0], conv_w_pw1[0], conv_b_pw1[0])
    u_s = sample_rows(u)
    y = _dwconv_prompt(u, conv_w_dw[0], conv_b_dw[0], n_seq=bp, seq_len=lp)
    y = _dwconv_sample(y, jnp.transpose(state_conv[0], (1, 0, 2)), u_s, conv_w_dw[0], conv_b_dw[0], row0=tp)
    x = _ln_pw2(y, x, conv_ln_g[0], conv_ln_b[0], conv_w_pw2[0], conv_b_pw2[0])
    x = ffn(x, 0, 1)
    new_conv_prompt = prompt_rows(u, lp - CONV_HALO)[None]
    new_conv_sample = jnp.concatenate([state_conv[0], jnp.transpose(u_s, (1, 0, 2))], axis=1)[None, :, ls:]

    x = ffn(x, 1, 0)
    p_qk, la = _gla_qk_decay(x, norm_mix[1], gla_w_q[0], gla_w_k[0], gla_w_g1[0], gla_w_g2[0], gla_b_g[0])
    p_v = _norm_proj(x, norm_mix[1], [gla_w_v[0]], name="gla_proj_v")
    p_r = _norm_proj(x, norm_mix[1], [gla_w_r[0]], name="gla_proj_r")
    og, new_gla_prompt = _gla_prompt(p_qk, p_v, p_r, la, gla_gn_g[0], n_seq=bp, seq_len=lp)

    def seq_major(a):
        return jnp.transpose(sample_rows(a), (1, 0, 2)).reshape(ts, -1)

    p_s = jnp.concatenate([seq_major(p_qk), seq_major(p_v), seq_major(p_r)], axis=1)
    og_s, new_gla_sample = _gla_sample(p_s, seq_major(la), state_gla[0], gla_gn_g[0], seq_len=ls)
    og_s = jnp.transpose(og_s.reshape(bs, ls, -1), (1, 0, 2)).reshape(ts, -1)
    og = lax.dynamic_update_slice(og, with_pad(og_s), (tp, 0))
    x = _out_proj(og, x, gla_w_o[0])

    tm_p = max(r for r in range(8, MAX_TOKEN_TILE + 1, 8) if sp % r == 0)
    per_seq = sp // tm_p
    y_prompt = _ffn(x, norm_ffn, w_ffn_gate, w_ffn_up, w_ffn_down, norm_final, layer=1, slot=1, final_norm=True,
                    windows=(tm_p, bp * per_seq, lambda i: (i // per_seq) * lp + N_META_ROWS + (i % per_seq) * tm_p))
    y_sample_t = _ffn(x, norm_ffn, w_ffn_gate, w_ffn_up, w_ffn_down, norm_final, layer=1, slot=1, final_norm=True,
                      windows=(ts, 1, lambda i: tp + 0 * i))
    y_prompt = y_prompt.reshape(bp, sp, d)
    y_sample = jnp.transpose(y_sample_t.reshape(ls, bs, d), (1, 0, 2))
    return (y_prompt, y_sample, new_conv_prompt, new_gla_prompt[None], new_conv_sample, new_gla_sample[None])
```

```python
import functools
import math

import jax
import jax.numpy as jnp
from jax import lax
from jax.experimental import pallas as pl
from jax.experimental.pallas import tpu as pltpu

F32 = jnp.float32
SUBLANES, LANES = 8, 128
EPS = 1e-6
N_META_ROWS = 16
CONV_TAPS = 31
CONV_HALO = CONV_TAPS - 1
HALO_PAD = 32
GLA_HEADS = 4
GLA_HEADS_PER_STEP = 4
GLA_SUB = 16
GLA_CHUNK = 128
GLA_TAU = 16.0
LOG2_E = 1.4426950408889634
V7X_VMEM_LIMIT = 60000 * 1024

TOKEN_ROW_MULTIPLE = 128
MAX_TOKEN_TILE = 1152
FFN_COL_TILES = (512, 256)
PROJ_GROUP = 2048
CONV_LANE_TILE = 256
CONV_SAMPLE_LANE_TILE = 1024
CONV_MAX_ROW_CHUNK = 344
GLA_MAX_ROW_BLOCK = 1024


def _token_tile(t_pad):
    nb = -(-t_pad // MAX_TOKEN_TILE)
    assert t_pad % (2 * SUBLANES * nb) == 0, t_pad
    return t_pad // nb


def _ffn_col_tile(tm, d, f):
    for tf in FFN_COL_TILES:
        if f % tf == 0 and 4 * (4 * tm * d + 6 * d * tf + 3 * tm * tf) <= V7X_VMEM_LIMIT:
            return tf
    raise ValueError((tm, d, f))


def _proj_tile(t_pad):
    return _token_tile(t_pad) // 2


def _params(*sem):
    return pltpu.CompilerParams(dimension_semantics=sem, vmem_limit_bytes=V7X_VMEM_LIMIT)


def _resident(shape, index_map):
    return pl.BlockSpec(shape, index_map, pipeline_mode=pl.Buffered(1))


def _rms(x, g):
    return x * lax.rsqrt(jnp.mean(x * x, axis=-1, keepdims=True) + EPS) * g


def _silu(x):
    return x * jax.nn.sigmoid(x)


def _dot(a, b):
    return jnp.dot(a, b, preferred_element_type=F32)


def _ffn_kernel(x_ref, g_ref, wg_ref, wu_ref, wd_ref, gf_ref, o_ref, h_ref, *, final_norm):
    f = pl.program_id(1)

    @pl.when(f == 0)
    def _():
        x = x_ref[...]
        h_ref[...] = _rms(x, g_ref[...])
        o_ref[...] = x

    h = h_ref[...]
    a = 0.5 * _silu(_dot(h, wg_ref[...])) * _dot(h, wu_ref[...])
    o_ref[...] += _dot(a, wd_ref[...])

    if final_norm:
        @pl.when(f == pl.num_programs(1) - 1)
        def _():
            o_ref[...] = _rms(o_ref[...], gf_ref[...])


def _ffn(x, g, wg, wu, wd, gf, *, layer, slot, final_norm, windows=None):
    t, d = x.shape
    f = wg.shape[-1]
    if windows is None:
        tm = _token_tile(t)
        n_tiles = t // tm
        x_spec = pl.BlockSpec((tm, d), lambda i, j: (i, 0))
    else:
        tm, n_tiles, row_start = windows
        x_spec = pl.BlockSpec((pl.Element(tm), pl.Element(d)),
                              lambda i, j: (pl.multiple_of(row_start(i), SUBLANES), 0))
    tf = _ffn_col_tile(tm, d, f)
    return pl.pallas_call(
        functools.partial(_ffn_kernel, final_norm=final_norm),
        grid=(n_tiles, f // tf),
        in_specs=[
            x_spec,
            pl.BlockSpec((None, None, 1, d), lambda i, j: (layer, slot, 0, 0)),
            pl.BlockSpec((None, None, d, tf), lambda i, j: (layer, slot, 0, j)),
            pl.BlockSpec((None, None, d, tf), lambda i, j: (layer, slot, 0, j)),
            pl.BlockSpec((None, None, tf, d), lambda i, j: (layer, slot, j, 0)),
            pl.BlockSpec((1, d), lambda i, j: (0, 0)),
        ],
        out_specs=pl.BlockSpec((tm, d), lambda i, j: (i, 0)),
        out_shape=jax.ShapeDtypeStruct((n_tiles * tm, d), F32),
        scratch_shapes=[pltpu.VMEM((tm, d), F32)],
        compiler_params=_params("parallel", "arbitrary"),
        name="ffn_final" if final_norm else "ffn",
    )(x, g.reshape(g.shape[0], g.shape[1], 1, d), wg, wu, wd, gf.reshape(1, d))


def _row_scale(x):
    return lax.rsqrt(jnp.mean(x * x, axis=-1, keepdims=True) + EPS)


def _pw1_kernel(x_ref, g_ref, wa_ref, wb_ref, ba_ref, bb_ref, u_ref):
    x = x_ref[...]
    xg = x * g_ref[...]
    rs = _row_scale(x)
    a = _dot(xg, wa_ref[...]) * rs + ba_ref[...]
    b = _dot(xg, wb_ref[...]) * rs + bb_ref[...]
    u_ref[...] = a * jax.nn.sigmoid(b)


def _pw1_glu(x, g, w, b):
    t, d = x.shape
    dc = w.shape[1] // 2
    tm, tn = _proj_tile(t), PROJ_GROUP // 2
    nc = dc // tn
    b2 = b.reshape(1, 2 * dc)
    return pl.pallas_call(
        _pw1_kernel,
        grid=(nc, t // tm),
        in_specs=[
            pl.BlockSpec((tm, d), lambda c, i: (i, 0)),
            pl.BlockSpec((1, d), lambda c, i: (0, 0)),
            _resident((d, tn), lambda c, i: (0, c)),
            _resident((d, tn), lambda c, i: (0, c + nc)),
            pl.BlockSpec((1, tn), lambda c, i: (0, c)),
            pl.BlockSpec((1, tn), lambda c, i: (0, c + nc)),
        ],
        out_specs=pl.BlockSpec((tm, tn), lambda c, i: (i, c)),
        out_shape=jax.ShapeDtypeStruct((t, dc), F32),
        compiler_params=_params("arbitrary", "arbitrary"),
        name="conv_pw1_glu",
    )(x, g.reshape(1, d), w, w, b2, b2)


def _conv_window(win, wb_ref, b_ref, rc):
    sl = SUBLANES
    lead = HALO_PAD - CONV_HALO
    tl = win.shape[1]
    n_tiles = rc // sl + 1
    acc = jnp.broadcast_to(b_ref[...], (rc, tl))
    for s in range(sl):
        part = None
        for a in range(-(-(CONV_TAPS + lead) // sl)):
            k = sl * a + s - lead
            if 0 <= k < CONV_TAPS:
                rows = win[sl * a:sl * a + rc + sl].reshape(n_tiles, sl, tl)
                term = (rows * wb_ref[k][None]).reshape(rc + sl, tl)
                part = term if part is None else part + term
        acc = acc + (part[:rc] if s == 0 else pltpu.roll(part, rc + sl - s, axis=0)[:rc])
    return acc


def _dwconv_prompt_kernel(cur_ref, w_ref, b_ref, y_ref, ext_ref, wb_ref, *, rc):
    lp, tl = cur_ref.shape
    ext_ref[0:HALO_PAD, :] = jnp.zeros((HALO_PAD, tl), F32)
    ext_ref[HALO_PAD:HALO_PAD + lp, :] = cur_ref[...]
    ext_ref[HALO_PAD + lp:HALO_PAD + lp + SUBLANES, :] = jnp.zeros((SUBLANES, tl), F32)
    for k in range(CONV_TAPS):
        wb_ref[k] = jnp.broadcast_to(w_ref[k:k + 1, :], (SUBLANES, tl))

    def body(c, carry):
        r0 = pl.multiple_of(c * rc, SUBLANES)
        y_ref[pl.ds(r0, rc), :] = _conv_window(ext_ref[pl.ds(r0, rc + HALO_PAD + SUBLANES), :], wb_ref, b_ref, rc)
        return carry

    lax.fori_loop(0, lp // rc, body, 0)


def _dwconv_prompt(u, w, b, *, n_seq, seq_len):
    d = u.shape[1]
    tl = CONV_LANE_TILE
    rc = max(r for r in range(SUBLANES, CONV_MAX_ROW_CHUNK + 1, SUBLANES) if seq_len % r == 0)
    return pl.pallas_call(
        functools.partial(_dwconv_prompt_kernel, rc=rc),
        grid=(n_seq, d // tl),
        in_specs=[
            pl.BlockSpec((seq_len, tl), lambda i, j: (i, j)),
            pl.BlockSpec((CONV_TAPS, tl), lambda i, j: (0, j)),
            pl.BlockSpec((1, tl), lambda i, j: (0, j)),
        ],
        out_specs=pl.BlockSpec((seq_len, tl), lambda i, j: (i, j)),
        out_shape=jax.ShapeDtypeStruct(u.shape, F32),
        scratch_shapes=[pltpu.VMEM((HALO_PAD + seq_len + SUBLANES, tl), F32),
                        pltpu.VMEM((CONV_TAPS, SUBLANES, tl), F32)],
        compiler_params=_params("parallel", "parallel"),
        name="dwconv_prompt",
    )(u, w, b.reshape(1, d))


def _dwconv_sample_kernel(st_ref, u_ref, w_ref, b_ref, y_hbm_ref, y_ref, ext_ref, *, n_conv_steps):
    del y_hbm_ref
    ls = u_ref.shape[0]
    q = pl.program_id(1)
    t = q % ls

    @pl.when((t == 0) & (q < n_conv_steps))
    def _():
        ext_ref[0:CONV_HALO] = st_ref[...]
        ext_ref[CONV_HALO:CONV_HALO + ls] = u_ref[...]

    @pl.when(q < n_conv_steps)
    def _():
        acc = jnp.broadcast_to(b_ref[...], y_ref.shape)
        for k in range(CONV_TAPS):
            acc = acc + w_ref[k:k + 1, :] * ext_ref[t + k]
        y_ref[...] = acc

    @pl.when(q >= n_conv_steps)
    def _():
        y_ref[...] = jnp.zeros_like(y_ref)


def _dwconv_sample(y_flat, state_t, u_t, w, b, *, row0):
    ls, bs, d = u_t.shape
    tl = CONV_SAMPLE_LANE_TILE
    sb = math.gcd(row0, bs)
    nsb = bs // sb
    n_pad = y_flat.shape[0] - row0 - ls * bs
    assert sb % SUBLANES == 0 and n_pad % sb == 0
    n_conv = ls * nsb

    def seq_block(q):
        return jnp.minimum(q // ls, nsb - 1)

    def out_block(q):
        return row0 // sb + jnp.where(q < n_conv, (q % ls) * nsb + q // ls, q)

    return pl.pallas_call(
        functools.partial(_dwconv_sample_kernel, n_conv_steps=n_conv),
        grid=(d // tl, n_conv + n_pad // sb),
        in_specs=[
            pl.BlockSpec((CONV_HALO, sb, tl), lambda j, q: (0, seq_block(q), j)),
            pl.BlockSpec((ls, sb, tl), lambda j, q: (0, seq_block(q), j)),
            pl.BlockSpec((CONV_TAPS, tl), lambda j, q: (0, j)),
            pl.BlockSpec((1, tl), lambda j, q: (0, j)),
            pl.BlockSpec(memory_space=pl.ANY),
        ],
        out_specs=pl.BlockSpec((sb, tl), lambda j, q: (out_block(q), j)),
        out_shape=jax.ShapeDtypeStruct(y_flat.shape, F32),
        scratch_shapes=[pltpu.VMEM((CONV_HALO + ls, sb, tl), F32)],
        input_output_aliases={4: 0},
        compiler_params=_params("parallel", "arbitrary"),
        name="dwconv_sample",
    )(state_t, u_t, w, b.reshape(1, d), y_flat)


def _ln_pw2_kernel(y_ref, x_ref, lg_ref, lb_ref, w_ref, b_ref, o_ref):
    y = y_ref[...]
    mu = jnp.mean(y, axis=-1, keepdims=True)
    yc = y - mu
    var = jnp.mean(yc * yc, axis=-1, keepdims=True)
    z = _silu(yc * lax.rsqrt(var + EPS) * lg_ref[...] + lb_ref[...])
    o_ref[...] = x_ref[...] + _dot(z, w_ref[...]) + b_ref[...]


def _ln_pw2(y, x, lg, lb, w, b):
    t, d = x.shape
    dc = y.shape[1]
    tm = _proj_tile(t)
    return pl.pallas_call(
        _ln_pw2_kernel,
        grid=(t // tm,),
        in_specs=[
            pl.BlockSpec((tm, dc), lambda i: (i, 0)),
            pl.BlockSpec((tm, d), lambda i: (i, 0)),
            pl.BlockSpec((1, dc), lambda i: (0, 0)),
            pl.BlockSpec((1, dc), lambda i: (0, 0)),
            _resident((dc, d), lambda i: (0, 0)),
            pl.BlockSpec((1, d), lambda i: (0, 0)),
        ],
        out_specs=pl.BlockSpec((tm, d), lambda i: (i, 0)),
        out_shape=jax.ShapeDtypeStruct((t, d), F32),
        compiler_params=_params("parallel"),
        name="conv_ln_pw2",
    )(y, x, lg.reshape(1, dc), lb.reshape(1, dc), w, b.reshape(1, d))


def _log_sigmoid(z):
    return jnp.minimum(z, 0.0) - jnp.log1p(jnp.exp(-jnp.abs(z)))


def _norm_proj_kernel(x_ref, g_ref, *refs):
    *w_refs, p_ref = refs
    x = x_ref[...]
    xg = x * g_ref[...]
    rs = _row_scale(x)
    parts = [_dot(xg, w_ref[...]) * rs for w_ref in w_refs]
    p_ref[...] = parts[0] if len(parts) == 1 else jnp.concatenate(parts, axis=1)


def _norm_proj(x, g, ws, *, name):
    t, d = x.shape
    n = sum(w.shape[1] for w in ws)
    tm = _proj_tile(t)
    return pl.pallas_call(
        _norm_proj_kernel,
        grid=(t // tm,),
        in_specs=[pl.BlockSpec((tm, d), lambda i: (i, 0)), pl.BlockSpec((1, d), lambda i: (0, 0))]
        + [_resident(w.shape, lambda i: (0, 0)) for w in ws],
        out_specs=pl.BlockSpec((tm, n), lambda i: (i, 0)),
        out_shape=jax.ShapeDtypeStruct((t, n), F32),
        compiler_params=_params("parallel"),
        name=name,
    )(x, g.reshape(1, d), *ws)


def _gla_qk_decay_kernel(x_ref, g_ref, wq_ref, wk_ref, wg1_ref, wg2_ref, bg_ref, p_ref, la_ref):
    x = x_ref[...]
    xg = x * g_ref[...]
    rs = _row_scale(x)
    p_ref[...] = jnp.concatenate([_dot(xg, wq_ref[...]) * rs, _dot(xg, wk_ref[...]) * rs], axis=1)
    z = _dot(_dot(xg, wg1_ref[...]) * rs, wg2_ref[...]) + bg_ref[...]
    la_ref[...] = _log_sigmoid(z) * (1.0 / GLA_TAU)


def _gla_qk_decay(x, g, wq, wk, wg1, wg2, bg):
    t, d = x.shape
    r, nk = wg2.shape
    n = wq.shape[1] + wk.shape[1]
    tm = _proj_tile(t)
    return pl.pallas_call(
        _gla_qk_decay_kernel,
        grid=(t // tm,),
        in_specs=[
            pl.BlockSpec((tm, d), lambda i: (i, 0)),
            pl.BlockSpec((1, d), lambda i: (0, 0)),
            _resident(wq.shape, lambda i: (0, 0)),
            _resident(wk.shape, lambda i: (0, 0)),
            _resident((d, r), lambda i: (0, 0)),
            _resident((r, nk), lambda i: (0, 0)),
            pl.BlockSpec((1, nk), lambda i: (0, 0)),
        ],
        out_specs=[pl.BlockSpec((tm, n), lambda i: (i, 0)), pl.BlockSpec((tm, nk), lambda i: (i, 0))],
        out_shape=[jax.ShapeDtypeStruct((t, n), F32), jax.ShapeDtypeStruct((t, nk), F32)],
        compiler_params=_params("parallel"),
        name="gla_proj_qk_decay",
    )(x, g.reshape(1, d), wq, wk, wg1, wg2, bg.reshape(1, nk))


def _bf16_limbs(x):
    hi = x.astype(jnp.bfloat16).astype(F32)
    mid = (x - hi).astype(jnp.bfloat16).astype(F32)
    return hi, mid, x - hi - mid


def _gla_gate(o, r, gn):
    return o * lax.rsqrt(jnp.mean(o * o, axis=-1, keepdims=True) + EPS) * gn * _silu(r)


def _gla_chunk(q, k, v, la, s0):
    c, dk = q.shape
    n_sub = c // GLA_SUB
    neg_inf = -jnp.inf
    row = lax.broadcasted_iota(jnp.int32, (c, c), 0)
    col = lax.broadcasted_iota(jnp.int32, (c, c), 1)
    tri = jnp.where(row >= col, 1.0, 0.0).astype(F32)
    b = _dot(jnp.concatenate([tri, tri, tri], axis=1), jnp.concatenate(_bf16_limbs(la), axis=0)) * LOG2_E

    o = _dot(q * jnp.exp2(b), s0)

    row_c = lax.broadcasted_iota(jnp.int32, (c, 1), 0)
    lane_c = lax.broadcasted_iota(jnp.int32, (GLA_SUB, c), 1)
    t_sub = lax.broadcasted_iota(jnp.int32, (GLA_SUB, 1), 0)
    blocks = []
    for i in range(n_sub):
        lo = GLA_SUB * i
        qi, ki, bi = q[lo:lo + GLA_SUB], k[lo:lo + GLA_SUB], b[lo:lo + GLA_SUB]
        if i == 0:
            sc = jnp.zeros((GLA_SUB, c), F32)
        else:
            b_start = b[lo - 1:lo]
            qt = qi * jnp.exp2(bi - b_start)
            kt = k * jnp.exp2(jnp.where(row_c < lo, b_start - b, neg_inf))
            sc = lax.dot_general(qt, kt, (((1,), (1,)), ((), ())), preferred_element_type=F32)
        for s in range(GLA_SUB):
            colv = jnp.sum(qi * jnp.exp2(bi - bi[s:s + 1]) * ki[s:s + 1], axis=-1, keepdims=True)
            sc = sc + jnp.where(lane_c == lo + s, jnp.where(t_sub >= s, colv, 0.0), 0.0)
        blocks.append(sc)
    scores = blocks[0] if n_sub == 1 else jnp.concatenate(blocks, axis=0)
    o = o + _dot(scores, v)

    b_last = b[c - 1:c]
    kd = k * jnp.exp2(b_last - b)
    upd = lax.dot_general(kd, v, (((0,), (0,)), ((), ())), preferred_element_type=F32)
    s_new = _column_scale(jnp.exp2(b_last), s0) + upd
    return o, s_new


def _column_scale(d_row, s):
    dk, dv = s.shape
    d_col = jnp.transpose(jnp.broadcast_to(d_row, (LANES, dk)))
    return jnp.concatenate([d_col] * (dv // LANES), axis=1) * s


def _gla_prompt_kernel(q_ref, k_ref, v_ref, r_ref, la_ref, gn_ref, og_ref, sn_ref, s_ref):
    rb = q_ref.shape[0]
    hps, dk, dv = s_ref.shape
    q_scale = dk ** -0.5
    j = pl.program_id(2)

    @pl.when(j == 0)
    def _():
        s_ref[...] = jnp.zeros_like(s_ref)

    def run(r0, c):
        rows = pl.ds(r0, c)
        for h in range(hps):
            kc, vc = slice(h * dk, (h + 1) * dk), slice(h * dv, (h + 1) * dv)
            o, s_new = _gla_chunk(q_ref[rows, kc] * q_scale, k_ref[rows, kc], v_ref[rows, vc], la_ref[rows, kc],
                                  s_ref[h])
            s_ref[h] = s_new
            og_ref[rows, vc] = _gla_gate(o, r_ref[rows, vc], gn_ref[...])

    n_full = rb // GLA_CHUNK

    def body(i, carry):
        run(pl.multiple_of(i * GLA_CHUNK, GLA_CHUNK), GLA_CHUNK)
        return carry

    lax.fori_loop(0, n_full, body, 0)
    if rb > n_full * GLA_CHUNK:
        run(n_full * GLA_CHUNK, rb - n_full * GLA_CHUNK)

    @pl.when(j == pl.num_programs(2) - 1)
    def _():
        sn_ref[0] = s_ref[...]


def _gla_prompt(qk, v, r, la, gn, *, n_seq, seq_len):
    h, hps = GLA_HEADS, GLA_HEADS_PER_STEP
    dk = la.shape[1] // h
    dv = gn.shape[0]
    rb = max(r_ for r_ in range(GLA_SUB, GLA_MAX_ROW_BLOCK + 1, GLA_SUB) if seq_len % r_ == 0)
    nrb, ng = seq_len // rb, h // hps
    assert dv == 2 * dk
    return pl.pallas_call(
        _gla_prompt_kernel,
        grid=(n_seq, ng, nrb),
        in_specs=[
            pl.BlockSpec((rb, hps * dk), lambda b, g, j: (b * nrb + j, g)),
            pl.BlockSpec((rb, hps * dk), lambda b, g, j: (b * nrb + j, ng + g)),
            pl.BlockSpec((rb, hps * dv), lambda b, g, j: (b * nrb + j, g)),
            pl.BlockSpec((rb, hps * dv), lambda b, g, j: (b * nrb + j, g)),
            pl.BlockSpec((rb, hps * dk), lambda b, g, j: (b * nrb + j, g)),
            pl.BlockSpec((1, dv), lambda b, g, j: (0, 0)),
        ],
        out_specs=[
            pl.BlockSpec((rb, hps * dv), lambda b, g, j: (b * nrb + j, g)),
            pl.BlockSpec((1, hps, dk, dv), lambda b, g, j: (b, g, 0, 0)),
        ],
        out_shape=[jax.ShapeDtypeStruct(v.shape, F32),
                   jax.ShapeDtypeStruct((n_seq, h, dk, dv), F32)],
        scratch_shapes=[pltpu.VMEM((hps, dk, dv), F32)],
        compiler_params=_params("parallel", "parallel", "arbitrary"),
        name="gla_prompt",
    )(qk, qk, v, r, la, gn.reshape(1, dv))


def _gla_sample_kernel(p_ref, la_ref, s_ref, gn_ref, og_ref, sn_ref, *, seq_len):
    rows = p_ref.shape[0]
    n_seq = rows // seq_len
    dk = la_ref.shape[1] // GLA_HEADS
    dv = 2 * dk
    q_scale = dk ** -0.5
    neg_inf = -jnp.inf
    t_row = lax.broadcasted_iota(jnp.int32, (rows, 1), 0)
    for hd in range(GLA_HEADS):
        q = p_ref[:, hd * dk:(hd + 1) * dk] * q_scale
        k = p_ref[:, (GLA_HEADS + hd) * dk:(GLA_HEADS + hd + 1) * dk]
        v = p_ref[:, (GLA_HEADS + hd) * dv:(GLA_HEADS + hd + 1) * dv]
        r = p_ref[:, (2 * GLA_HEADS + hd) * dv:(2 * GLA_HEADS + hd + 1) * dv]
        la = la_ref[:, hd * dk:(hd + 1) * dk]

        def in_seq_from(s):
            return (t_row >= s) & (t_row < (s // seq_len + 1) * seq_len)

        b = jnp.zeros_like(la)
        for s in range(rows):
            b = b + jnp.where(in_seq_from(s), la[s:s + 1], 0.0)

        qe = q * jnp.exp(b)
        o = jnp.zeros((rows, dv), F32)
        for g in range(n_seq):
            mine = (t_row >= g * seq_len) & (t_row < (g + 1) * seq_len)
            o = jnp.where(mine, _dot(qe, s_ref[g, hd]), o)
        for s in range(rows):
            dec = jnp.exp(jnp.where(in_seq_from(s), b - b[s:s + 1], neg_inf))
            colv = jnp.sum(q * dec * k[s:s + 1], axis=-1, keepdims=True)
            o = o + colv * v[s:s + 1]
        og_ref[:, hd * dv:(hd + 1) * dv] = _gla_gate(o, r, gn_ref[...])

        for g in range(n_seq):
            last = (g + 1) * seq_len - 1
            b_last = b[last:last + 1]
            mine = (t_row >= g * seq_len) & (t_row < (g + 1) * seq_len)
            kd = k * jnp.exp(jnp.where(mine, b_last - b, neg_inf))
            upd = lax.dot_general(kd, v, (((0,), (0,)), ((), ())), preferred_element_type=F32)
            sn_ref[g, hd] = _column_scale(jnp.exp(b_last), s_ref[g, hd]) + upd


def _gla_sample(p_s, la_s, state, gn, *, seq_len):
    rows = SUBLANES
    n, h, dk, dv = state.shape
    t = p_s.shape[0]
    per = rows // seq_len
    return pl.pallas_call(
        functools.partial(_gla_sample_kernel, seq_len=seq_len),
        grid=(t // rows,),
        in_specs=[
            pl.BlockSpec((rows, p_s.shape[1]), lambda i: (i, 0)),
            pl.BlockSpec((rows, la_s.shape[1]), lambda i: (i, 0)),
            pl.BlockSpec((per, h, dk, dv), lambda i: (i, 0, 0, 0)),
            pl.BlockSpec((1, dv), lambda i: (0, 0)),
        ],
        out_specs=[
            pl.BlockSpec((rows, h * dv), lambda i: (i, 0)),
            pl.BlockSpec((per, h, dk, dv), lambda i: (i, 0, 0, 0)),
        ],
        out_shape=[jax.ShapeDtypeStruct((t, h * dv), F32), jax.ShapeDtypeStruct(state.shape, F32)],
        compiler_params=_params("parallel"),
        name="gla_sample",
    )(p_s, la_s, state, gn.reshape(1, dv))


def _out_proj_kernel(a_ref, x_ref, w_ref, o_ref):
    o_ref[...] = x_ref[...] + _dot(a_ref[...], w_ref[...])


def _out_proj(a, x, w):
    t, d = x.shape
    kdim = a.shape[1]
    tm = _proj_tile(t)
    return pl.pallas_call(
        _out_proj_kernel,
        grid=(t // tm,),
        in_specs=[
            pl.BlockSpec((tm, kdim), lambda i: (i, 0)),
            pl.BlockSpec((tm, d), lambda i: (i, 0)),
            _resident((kdim, d), lambda i: (0, 0)),
        ],
        out_specs=pl.BlockSpec((tm, d), lambda i: (i, 0)),
        out_shape=jax.ShapeDtypeStruct((t, d), F32),
        compiler_params=_params("parallel"),
        name="gla_out_proj",
    )(a, x, w)


def kernel(x_prompt, x_sample, state_conv, state_gla, meta_tokens, norm_ffn, w_ffn_gate, w_ffn_up, w_ffn_down,
           norm_mix, conv_w_pw1, conv_b_pw1, conv_w_dw, conv_b_dw, conv_ln_g, conv_ln_b, conv_w_pw2, conv_b_pw2,
           gla_w_q, gla_w_k, gla_w_v, gla_w_g1, gla_w_g2, gla_b_g, gla_w_r, gla_gn_g, gla_w_o, norm_final):
    bp, sp, d = x_prompt.shape
    bs, ls, _ = x_sample.shape
    lp = N_META_ROWS + sp
    tp, ts = bp * lp, bs * ls
    n_pad = -(tp + ts) % TOKEN_ROW_MULTIPLE
    assert SUBLANES % ls == 0 and ts % SUBLANES == 0

    def with_pad(a):
        return jnp.concatenate([a, jnp.zeros((n_pad, a.shape[1]), F32)], axis=0) if n_pad else a

    x = jnp.concatenate(
        [piece for b in range(bp) for piece in (meta_tokens, x_prompt[b])]
        + [with_pad(jnp.transpose(x_sample, (1, 0, 2)).reshape(ts, d))], axis=0)

    def ffn(x, i, j, final_norm=False):
        return _ffn(x, norm_ffn, w_ffn_gate, w_ffn_up, w_ffn_down, norm_final,
                    layer=i, slot=j, final_norm=final_norm)

    def prompt_rows(a, start):
        return jnp.stack([a[b * lp + start:(b + 1) * lp] for b in range(bp)])

    def sample_rows(a):
        return a[tp:tp + ts].reshape(ls, bs, -1)

    x = ffn(x, 0, 0)
    u = _pw1_glu(x, norm_mix[0], conv_w_pw1[0], conv_b_pw1[0])
    u_s = sample_rows(u)
    y = _dwconv_prompt(u, conv_w_dw[0], conv_b_dw[0], n_seq=bp, seq_len=lp)
    y = _dwconv_sample(y, jnp.transpose(state_conv[0], (1, 0, 2)), u_s, conv_w_dw[0], conv_b_dw[0], row0=tp)
    x = _ln_pw2(y, x, conv_ln_g[0], conv_ln_b[0], conv_w_pw2[0], conv_b_pw2[0])
    x = ffn(x, 0, 1)
    new_conv_prompt = prompt_rows(u, lp - CONV_HALO)[None]
    new_conv_sample = jnp.concatenate([state_conv[0], jnp.transpose(u_s, (1, 0, 2))], axis=1)[None, :, ls:]

    x = ffn(x, 1, 0)
    p_qk, la = _gla_qk_decay(x, norm_mix[1], gla_w_q[0], gla_w_k[0], gla_w_g1[0], gla_w_g2[0], gla_b_g[0])
    p_v = _norm_proj(x, norm_mix[1], [gla_w_v[0]], name="gla_proj_v")
    p_r = _norm_proj(x, norm_mix[1], [gla_w_r[0]], name="gla_proj_r")
    og, new_gla_prompt = _gla_prompt(p_qk, p_v, p_r, la, gla_gn_g[0], n_seq=bp, seq_len=lp)

    def seq_major(a):
        return jnp.transpose(sample_rows(a), (1, 0, 2)).reshape(ts, -1)

    p_s = jnp.concatenate([seq_major(p_qk), seq_major(p_v), seq_major(p_r)], axis=1)
    og_s, new_gla_sample = _gla_sample(p_s, seq_major(la), state_gla[0], gla_gn_g[0], seq_len=ls)
    og_s = jnp.transpose(og_s.reshape(bs, ls, -1), (1, 0, 2)).reshape(ts, -1)
    og = lax.dynamic_update_slice(og, with_pad(og_s), (tp, 0))
    x = _out_proj(og, x, gla_w_o[0])

    tm_p = max(r for r in range(SUBLANES, MAX_TOKEN_TILE + 1, SUBLANES) if sp % r == 0)
    per_seq = sp // tm_p
    y_prompt = _ffn(x, norm_ffn, w_ffn_gate, w_ffn_up, w_ffn_down, norm_final, layer=1, slot=1, final_norm=True,
                    windows=(tm_p, bp * per_seq, lambda i: (i // per_seq) * lp + N_META_ROWS + (i % per_seq) * tm_p))
    y_sample_t = _ffn(x, norm_ffn, w_ffn_gate, w_ffn_up, w_ffn_down, norm_final, layer=1, slot=1, final_norm=True,
                      windows=(ts, 1, lambda i: tp + 0 * i))
    y_prompt = y_prompt.reshape(bp, sp, d)
    y_sample = jnp.transpose(y_sample_t.reshape(ls, bs, d), (1, 0, 2))
    return (y_prompt, y_sample, new_conv_prompt, new_gla_prompt[None], new_conv_sample, new_gla_sample[None])
```

```python
import functools
import math

import jax
import jax.numpy as jnp
from jax import lax
from jax.experimental import pallas as pl
from jax.experimental.pallas import tpu as pltpu

F32 = jnp.float32
SUBLANES, LANES = 8, 128
EPS = 1e-6
N_META_ROWS = 16
CONV_TAPS = 31
CONV_HALO = CONV_TAPS - 1
HALO_PAD = 32
GLA_HEADS = 4
GLA_HEADS_PER_STEP = 4
GLA_SUB = 16
GLA_CHUNK = 128
GLA_TAU = 16.0
LOG2_E = 1.4426950408889634
V7X_VMEM_LIMIT = 60000 * 1024

TOKEN_ROW_MULTIPLE = 128
MAX_TOKEN_TILE = 1152
FFN_COL_TILES = (512, 256)
PROJ_GROUP = 2048
CONV_LANE_TILE = 256
CONV_SAMPLE_LANE_TILE = 1024
CONV_MAX_ROW_CHUNK = 344
GLA_MAX_ROW_BLOCK = 1024


def _token_tile(t_pad):
    nb = -(-t_pad // MAX_TOKEN_TILE)
    assert t_pad % (2 * SUBLANES * nb) == 0, t_pad
    return t_pad // nb


def _ffn_col_tile(tm, d, f):
    for tf in FFN_COL_TILES:
        if f % tf == 0 and 4 * (4 * tm * d + 6 * d * tf + 3 * tm * tf) <= V7X_VMEM_LIMIT:
            return tf
    raise ValueError((tm, d, f))


def _proj_tile(t_pad):
    return _token_tile(t_pad) // 2


def _params(*sem):
    return pltpu.CompilerParams(dimension_semantics=sem, vmem_limit_bytes=V7X_VMEM_LIMIT)


def _resident(shape, index_map):
    return pl.BlockSpec(shape, index_map, pipeline_mode=pl.Buffered(1))


def _rms(x, g):
    return x * lax.rsqrt(jnp.mean(x * x, axis=-1, keepdims=True) + EPS) * g


def _silu(x):
    return x * jax.nn.sigmoid(x)


def _dot(a, b):
    return jnp.dot(a, b, preferred_element_type=F32)


def _ffn_kernel(x_ref, g_ref, wg_ref, wu_ref, wd_ref, gf_ref, o_ref, h_ref, *, final_norm):
    f = pl.program_id(1)

    @pl.when(f == 0)
    def _():
        x = x_ref[...]
        h_ref[...] = _rms(x, g_ref[...])
        o_ref[...] = x

    h = h_ref[...]
    a = 0.5 * _silu(_dot(h, wg_ref[...])) * _dot(h, wu_ref[...])
    o_ref[...] += _dot(a, wd_ref[...])

    if final_norm:
        @pl.when(f == pl.num_programs(1) - 1)
        def _():
            o_ref[...] = _rms(o_ref[...], gf_ref[...])


def _ffn(x, g, wg, wu, wd, gf, *, layer, slot, final_norm, windows=None):
    t, d = x.shape
    f = wg.shape[-1]
    if windows is None:
        windows = (_token_tile(t), t // _token_tile(t), None, None, t)
    tm, n_tiles, in_start, out_start, out_rows = windows

    def row_spec(start):
        if start is None:
            return pl.BlockSpec((tm, d), lambda i, j: (i, 0))
        return pl.BlockSpec((pl.Element(tm), pl.Element(d)), lambda i, j: (pl.multiple_of(start(i), SUBLANES), 0))

    x_spec = row_spec(in_start)
    tf = _ffn_col_tile(tm, d, f)
    return pl.pallas_call(
        functools.partial(_ffn_kernel, final_norm=final_norm),
        grid=(n_tiles, f // tf),
        in_specs=[
            x_spec,
            pl.BlockSpec((None, None, 1, d), lambda i, j: (layer, slot, 0, 0)),
            pl.BlockSpec((None, None, d, tf), lambda i, j: (layer, slot, 0, j)),
            pl.BlockSpec((None, None, d, tf), lambda i, j: (layer, slot, 0, j)),
            pl.BlockSpec((None, None, tf, d), lambda i, j: (layer, slot, j, 0)),
            pl.BlockSpec((1, d), lambda i, j: (0, 0)),
        ],
        out_specs=row_spec(out_start),
        out_shape=jax.ShapeDtypeStruct((out_rows, d), F32),
        scratch_shapes=[pltpu.VMEM((tm, d), F32)],
        compiler_params=_params("parallel", "arbitrary"),
        name="ffn_final" if final_norm else "ffn",
    )(x, g.reshape(g.shape[0], g.shape[1], 1, d), wg, wu, wd, gf.reshape(1, d))


def _row_scale(x):
    return lax.rsqrt(jnp.mean(x * x, axis=-1, keepdims=True) + EPS)


def _pw1_kernel(x_ref, g_ref, wa_ref, wb_ref, ba_ref, bb_ref, u_ref):
    x = x_ref[...]
    xg = x * g_ref[...]
    rs = _row_scale(x)
    a = _dot(xg, wa_ref[...]) * rs + ba_ref[...]
    b = _dot(xg, wb_ref[...]) * rs + bb_ref[...]
    u_ref[...] = a * jax.nn.sigmoid(b)


def _pw1_glu(x, g, w, b):
    t, d = x.shape
    dc = w.shape[1] // 2
    tm, tn = _proj_tile(t), PROJ_GROUP // 2
    nc = dc // tn
    b2 = b.reshape(1, 2 * dc)
    return pl.pallas_call(
        _pw1_kernel,
        grid=(nc, t // tm),
        in_specs=[
            pl.BlockSpec((tm, d), lambda c, i: (i, 0)),
            pl.BlockSpec((1, d), lambda c, i: (0, 0)),
            _resident((d, tn), lambda c, i: (0, c)),
            _resident((d, tn), lambda c, i: (0, c + nc)),
            pl.BlockSpec((1, tn), lambda c, i: (0, c)),
            pl.BlockSpec((1, tn), lambda c, i: (0, c + nc)),
        ],
        out_specs=pl.BlockSpec((tm, tn), lambda c, i: (i, c)),
        out_shape=jax.ShapeDtypeStruct((t, dc), F32),
        compiler_params=_params("arbitrary", "arbitrary"),
        name="conv_pw1_glu",
    )(x, g.reshape(1, d), w, w, b2, b2)


def _conv_window(win, wb_ref, b_ref, rc):
    sl = SUBLANES
    lead = HALO_PAD - CONV_HALO
    tl = win.shape[1]
    n_tiles = rc // sl + 1
    acc = jnp.broadcast_to(b_ref[...], (rc, tl))
    for s in range(sl):
        part = None
        for a in range(-(-(CONV_TAPS + lead) // sl)):
            k = sl * a + s - lead
            if 0 <= k < CONV_TAPS:
                rows = win[sl * a:sl * a + rc + sl].reshape(n_tiles, sl, tl)
                term = (rows * wb_ref[k][None]).reshape(rc + sl, tl)
                part = term if part is None else part + term
        acc = acc + (part[:rc] if s == 0 else pltpu.roll(part, rc + sl - s, axis=0)[:rc])
    return acc


def _dwconv_prompt_kernel(cur_ref, w_ref, b_ref, y_ref, ext_ref, wb_ref, *, rc):
    lp, tl = cur_ref.shape
    ext_ref[0:HALO_PAD, :] = jnp.zeros((HALO_PAD, tl), F32)
    ext_ref[HALO_PAD:HALO_PAD + lp, :] = cur_ref[...]
    ext_ref[HALO_PAD + lp:HALO_PAD + lp + SUBLANES, :] = jnp.zeros((SUBLANES, tl), F32)
    for k in range(CONV_TAPS):
        wb_ref[k] = jnp.broadcast_to(w_ref[k:k + 1, :], (SUBLANES, tl))

    def body(c, carry):
        r0 = pl.multiple_of(c * rc, SUBLANES)
        y_ref[pl.ds(r0, rc), :] = _conv_window(ext_ref[pl.ds(r0, rc + HALO_PAD + SUBLANES), :], wb_ref, b_ref, rc)
        return carry

    lax.fori_loop(0, lp // rc, body, 0)


def _dwconv_prompt(u, w, b, *, n_seq, seq_len):
    d = u.shape[1]
    tl = CONV_LANE_TILE
    rc = max(r for r in range(SUBLANES, CONV_MAX_ROW_CHUNK + 1, SUBLANES) if seq_len % r == 0)
    return pl.pallas_call(
        functools.partial(_dwconv_prompt_kernel, rc=rc),
        grid=(n_seq, d // tl),
        in_specs=[
            pl.BlockSpec((seq_len, tl), lambda i, j: (i, j)),
            pl.BlockSpec((CONV_TAPS, tl), lambda i, j: (0, j)),
            pl.BlockSpec((1, tl), lambda i, j: (0, j)),
        ],
        out_specs=pl.BlockSpec((seq_len, tl), lambda i, j: (i, j)),
        out_shape=jax.ShapeDtypeStruct(u.shape, F32),
        scratch_shapes=[pltpu.VMEM((HALO_PAD + seq_len + SUBLANES, tl), F32),
                        pltpu.VMEM((CONV_TAPS, SUBLANES, tl), F32)],
        compiler_params=_params("parallel", "parallel"),
        name="dwconv_prompt",
    )(u, w, b.reshape(1, d))


def _dwconv_sample_kernel(st_ref, u_ref, w_ref, b_ref, y_hbm_ref, y_ref, ext_ref, *, n_conv_steps):
    del y_hbm_ref
    ls = u_ref.shape[0]
    q = pl.program_id(1)
    t = q % ls

    @pl.when((t == 0) & (q < n_conv_steps))
    def _():
        ext_ref[0:CONV_HALO] = st_ref[...]
        ext_ref[CONV_HALO:CONV_HALO + ls] = u_ref[...]

    @pl.when(q < n_conv_steps)
    def _():
        acc = jnp.broadcast_to(b_ref[...], y_ref.shape)
        for k in range(CONV_TAPS):
            acc = acc + w_ref[k:k + 1, :] * ext_ref[t + k]
        y_ref[...] = acc

    @pl.when(q >= n_conv_steps)
    def _():
        y_ref[...] = jnp.zeros_like(y_ref)


def _dwconv_sample(y_flat, state_t, u_t, w, b, *, row0):
    ls, bs, d = u_t.shape
    tl = CONV_SAMPLE_LANE_TILE
    sb = math.gcd(row0, bs)
    nsb = bs // sb
    n_pad = y_flat.shape[0] - row0 - ls * bs
    assert sb % SUBLANES == 0 and n_pad % sb == 0
    n_conv = ls * nsb

    def seq_block(q):
        return jnp.minimum(q // ls, nsb - 1)

    def out_block(q):
        return row0 // sb + jnp.where(q < n_conv, (q % ls) * nsb + q // ls, q)

    return pl.pallas_call(
        functools.partial(_dwconv_sample_kernel, n_conv_steps=n_conv),
        grid=(d // tl, n_conv + n_pad // sb),
        in_specs=[
            pl.BlockSpec((CONV_HALO, sb, tl), lambda j, q: (0, seq_block(q), j)),
            pl.BlockSpec((ls, sb, tl), lambda j, q: (0, seq_block(q), j)),
            pl.BlockSpec((CONV_TAPS, tl), lambda j, q: (0, j)),
            pl.BlockSpec((1, tl), lambda j, q: (0, j)),
            pl.BlockSpec(memory_space=pl.ANY),
        ],
        out_specs=pl.BlockSpec((sb, tl), lambda j, q: (out_block(q), j)),
        out_shape=jax.ShapeDtypeStruct(y_flat.shape, F32),
        scratch_shapes=[pltpu.VMEM((CONV_HALO + ls, sb, tl), F32)],
        input_output_aliases={4: 0},
        compiler_params=_params("parallel", "arbitrary"),
        name="dwconv_sample",
    )(state_t, u_t, w, b.reshape(1, d), y_flat)


def _ln_pw2_kernel(y_ref, x_ref, lg_ref, lb_ref, w_ref, b_ref, o_ref):
    y = y_ref[...]
    mu = jnp.mean(y, axis=-1, keepdims=True)
    yc = y - mu
    var = jnp.mean(yc * yc, axis=-1, keepdims=True)
    z = _silu(yc * lax.rsqrt(var + EPS) * lg_ref[...] + lb_ref[...])
    o_ref[...] = x_ref[...] + _dot(z, w_ref[...]) + b_ref[...]


def _ln_pw2(y, x, lg, lb, w, b):
    t, d = x.shape
    dc = y.shape[1]
    tm = _proj_tile(t)
    return pl.pallas_call(
        _ln_pw2_kernel,
        grid=(t // tm,),
        in_specs=[
            pl.BlockSpec((tm, dc), lambda i: (i, 0)),
            pl.BlockSpec((tm, d), lambda i: (i, 0)),
            pl.BlockSpec((1, dc), lambda i: (0, 0)),
            pl.BlockSpec((1, dc), lambda i: (0, 0)),
            _resident((dc, d), lambda i: (0, 0)),
            pl.BlockSpec((1, d), lambda i: (0, 0)),
        ],
        out_specs=pl.BlockSpec((tm, d), lambda i: (i, 0)),
        out_shape=jax.ShapeDtypeStruct((t, d), F32),
        compiler_params=_params("parallel"),
        name="conv_ln_pw2",
    )(y, x, lg.reshape(1, dc), lb.reshape(1, dc), w, b.reshape(1, d))


def _log_sigmoid(z):
    return jnp.minimum(z, 0.0) - jnp.log1p(jnp.exp(-jnp.abs(z)))


def _norm_proj_kernel(x_ref, g_ref, *refs):
    *w_refs, p_ref = refs
    x = x_ref[...]
    xg = x * g_ref[...]
    rs = _row_scale(x)
    parts = [_dot(xg, w_ref[...]) * rs for w_ref in w_refs]
    p_ref[...] = parts[0] if len(parts) == 1 else jnp.concatenate(parts, axis=1)


def _norm_proj(x, g, ws, *, name):
    t, d = x.shape
    n = sum(w.shape[1] for w in ws)
    tm = _proj_tile(t)
    return pl.pallas_call(
        _norm_proj_kernel,
        grid=(t // tm,),
        in_specs=[pl.BlockSpec((tm, d), lambda i: (i, 0)), pl.BlockSpec((1, d), lambda i: (0, 0))]
        + [_resident(w.shape, lambda i: (0, 0)) for w in ws],
        out_specs=pl.BlockSpec((tm, n), lambda i: (i, 0)),
        out_shape=jax.ShapeDtypeStruct((t, n), F32),
        compiler_params=_params("parallel"),
        name=name,
    )(x, g.reshape(1, d), *ws)


def _gla_qk_decay_kernel(x_ref, g_ref, wq_ref, wk_ref, wg1_ref, wg2_ref, bg_ref, p_ref, la_ref):
    x = x_ref[...]
    xg = x * g_ref[...]
    rs = _row_scale(x)
    p_ref[...] = jnp.concatenate([_dot(xg, wq_ref[...]) * rs, _dot(xg, wk_ref[...]) * rs], axis=1)
    z = _dot(_dot(xg, wg1_ref[...]) * rs, wg2_ref[...]) + bg_ref[...]
    la_ref[...] = _log_sigmoid(z) * (1.0 / GLA_TAU)


def _gla_qk_decay(x, g, wq, wk, wg1, wg2, bg):
    t, d = x.shape
    r, nk = wg2.shape
    n = wq.shape[1] + wk.shape[1]
    tm = _proj_tile(t)
    return pl.pallas_call(
        _gla_qk_decay_kernel,
        grid=(t // tm,),
        in_specs=[
            pl.BlockSpec((tm, d), lambda i: (i, 0)),
            pl.BlockSpec((1, d), lambda i: (0, 0)),
            _resident(wq.shape, lambda i: (0, 0)),
            _resident(wk.shape, lambda i: (0, 0)),
            _resident((d, r), lambda i: (0, 0)),
            _resident((r, nk), lambda i: (0, 0)),
            pl.BlockSpec((1, nk), lambda i: (0, 0)),
        ],
        out_specs=[pl.BlockSpec((tm, n), lambda i: (i, 0)), pl.BlockSpec((tm, nk), lambda i: (i, 0))],
        out_shape=[jax.ShapeDtypeStruct((t, n), F32), jax.ShapeDtypeStruct((t, nk), F32)],
        compiler_params=_params("parallel"),
        name="gla_proj_qk_decay",
    )(x, g.reshape(1, d), wq, wk, wg1, wg2, bg.reshape(1, nk))


def _bf16_limbs(x):
    hi = x.astype(jnp.bfloat16).astype(F32)
    mid = (x - hi).astype(jnp.bfloat16).astype(F32)
    return hi, mid, x - hi - mid


def _gla_gate(o, r, gn):
    return o * lax.rsqrt(jnp.mean(o * o, axis=-1, keepdims=True) + EPS) * gn * _silu(r)


def _gla_chunk(q, k, v, la, s0):
    c, dk = q.shape
    n_sub = c // GLA_SUB
    neg_inf = -jnp.inf
    row = lax.broadcasted_iota(jnp.int32, (c, c), 0)
    col = lax.broadcasted_iota(jnp.int32, (c, c), 1)
    tri = jnp.where(row >= col, 1.0, 0.0).astype(F32)
    b = _dot(jnp.concatenate([tri, tri, tri], axis=1), jnp.concatenate(_bf16_limbs(la), axis=0)) * LOG2_E

    o = _dot(q * jnp.exp2(b), s0)

    row_c = lax.broadcasted_iota(jnp.int32, (c, 1), 0)
    lane_c = lax.broadcasted_iota(jnp.int32, (GLA_SUB, c), 1)
    t_sub = lax.broadcasted_iota(jnp.int32, (GLA_SUB, 1), 0)
    blocks = []
    for i in range(n_sub):
        lo = GLA_SUB * i
        qi, ki, bi = q[lo:lo + GLA_SUB], k[lo:lo + GLA_SUB], b[lo:lo + GLA_SUB]
        if i == 0:
            sc = jnp.zeros((GLA_SUB, c), F32)
        else:
            b_start = b[lo - 1:lo]
            qt = qi * jnp.exp2(bi - b_start)
            kt = k * jnp.exp2(jnp.where(row_c < lo, b_start - b, neg_inf))
            sc = lax.dot_general(qt, kt, (((1,), (1,)), ((), ())), preferred_element_type=F32)
        for s in range(GLA_SUB):
            colv = jnp.sum(qi * jnp.exp2(bi - bi[s:s + 1]) * ki[s:s + 1], axis=-1, keepdims=True)
            sc = sc + jnp.where(lane_c == lo + s, jnp.where(t_sub >= s, colv, 0.0), 0.0)
        blocks.append(sc)
    scores = blocks[0] if n_sub == 1 else jnp.concatenate(blocks, axis=0)
    o = o + _dot(scores, v)

    b_last = b[c - 1:c]
    kd = k * jnp.exp2(b_last - b)
    upd = lax.dot_general(kd, v, (((0,), (0,)), ((), ())), preferred_element_type=F32)
    s_new = _column_scale(jnp.exp2(b_last), s0) + upd
    return o, s_new


def _column_scale(d_row, s):
    dk, dv = s.shape
    d_col = jnp.transpose(jnp.broadcast_to(d_row, (LANES, dk)))
    return jnp.concatenate([d_col] * (dv // LANES), axis=1) * s


def _gla_prompt_kernel(q_ref, k_ref, v_ref, r_ref, la_ref, gn_ref, og_ref, sn_ref, s_ref):
    rb = q_ref.shape[0]
    hps, dk, dv = s_ref.shape
    q_scale = dk ** -0.5
    j = pl.program_id(2)

    @pl.when(j == 0)
    def _():
        s_ref[...] = jnp.zeros_like(s_ref)

    def run(r0, c):
        rows = pl.ds(r0, c)
        for h in range(hps):
            kc, vc = slice(h * dk, (h + 1) * dk), slice(h * dv, (h + 1) * dv)
            o, s_new = _gla_chunk(q_ref[rows, kc] * q_scale, k_ref[rows, kc], v_ref[rows, vc], la_ref[rows, kc],
                                  s_ref[h])
            s_ref[h] = s_new
            og_ref[rows, vc] = _gla_gate(o, r_ref[rows, vc], gn_ref[...])

    n_full = rb // GLA_CHUNK

    def body(i, carry):
        run(pl.multiple_of(i * GLA_CHUNK, GLA_CHUNK), GLA_CHUNK)
        return carry

    lax.fori_loop(0, n_full, body, 0)
    if rb > n_full * GLA_CHUNK:
        run(n_full * GLA_CHUNK, rb - n_full * GLA_CHUNK)

    @pl.when(j == pl.num_programs(2) - 1)
    def _():
        sn_ref[0] = s_ref[...]


def _gla_prompt(qk, v, r, la, gn, *, n_seq, seq_len):
    h, hps = GLA_HEADS, GLA_HEADS_PER_STEP
    dk = la.shape[1] // h
    dv = gn.shape[0]
    rb = max(r_ for r_ in range(GLA_SUB, GLA_MAX_ROW_BLOCK + 1, GLA_SUB) if seq_len % r_ == 0)
    nrb, ng = seq_len // rb, h // hps
    assert dv == 2 * dk
    return pl.pallas_call(
        _gla_prompt_kernel,
        grid=(n_seq, ng, nrb),
        in_specs=[
            pl.BlockSpec((rb, hps * dk), lambda b, g, j: (b * nrb + j, g)),
            pl.BlockSpec((rb, hps * dk), lambda b, g, j: (b * nrb + j, ng + g)),
            pl.BlockSpec((rb, hps * dv), lambda b, g, j: (b * nrb + j, g)),
            pl.BlockSpec((rb, hps * dv), lambda b, g, j: (b * nrb + j, g)),
            pl.BlockSpec((rb, hps * dk), lambda b, g, j: (b * nrb + j, g)),
            pl.BlockSpec((1, dv), lambda b, g, j: (0, 0)),
        ],
        out_specs=[
            pl.BlockSpec((rb, hps * dv), lambda b, g, j: (b * nrb + j, g)),
            pl.BlockSpec((1, hps, dk, dv), lambda b, g, j: (b, g, 0, 0)),
        ],
        out_shape=[jax.ShapeDtypeStruct(v.shape, F32),
                   jax.ShapeDtypeStruct((n_seq, h, dk, dv), F32)],
        scratch_shapes=[pltpu.VMEM((hps, dk, dv), F32)],
        compiler_params=_params("parallel", "parallel", "arbitrary"),
        name="gla_prompt",
    )(qk, qk, v, r, la, gn.reshape(1, dv))


def _gla_sample_kernel(p_ref, la_ref, s_ref, gn_ref, og_ref, sn_ref, *, seq_len):
    rows = p_ref.shape[0]
    n_seq = rows // seq_len
    dk = la_ref.shape[1] // GLA_HEADS
    dv = 2 * dk
    q_scale = dk ** -0.5
    neg_inf = -jnp.inf
    t_row = lax.broadcasted_iota(jnp.int32, (rows, 1), 0)
    for hd in range(GLA_HEADS):
        q = p_ref[:, hd * dk:(hd + 1) * dk] * q_scale
        k = p_ref[:, (GLA_HEADS + hd) * dk:(GLA_HEADS + hd + 1) * dk]
        v = p_ref[:, (GLA_HEADS + hd) * dv:(GLA_HEADS + hd + 1) * dv]
        r = p_ref[:, (2 * GLA_HEADS + hd) * dv:(2 * GLA_HEADS + hd + 1) * dv]
        la = la_ref[:, hd * dk:(hd + 1) * dk]

        def in_seq_from(s):
            return (t_row >= s) & (t_row < (s // seq_len + 1) * seq_len)

        b = jnp.zeros_like(la)
        for s in range(rows):
            b = b + jnp.where(in_seq_from(s), la[s:s + 1], 0.0)

        qe = q * jnp.exp(b)
        o = jnp.zeros((rows, dv), F32)
        for g in range(n_seq):
            mine = (t_row >= g * seq_len) & (t_row < (g + 1) * seq_len)
            o = jnp.where(mine, _dot(qe, s_ref[g, hd]), o)
        for s in range(rows):
            dec = jnp.exp(jnp.where(in_seq_from(s), b - b[s:s + 1], neg_inf))
            colv = jnp.sum(q * dec * k[s:s + 1], axis=-1, keepdims=True)
            o = o + colv * v[s:s + 1]
        og_ref[:, hd * dv:(hd + 1) * dv] = _gla_gate(o, r, gn_ref[...])

        for g in range(n_seq):
            last = (g + 1) * seq_len - 1
            b_last = b[last:last + 1]
            mine = (t_row >= g * seq_len) & (t_row < (g + 1) * seq_len)
            kd = k * jnp.exp(jnp.where(mine, b_last - b, neg_inf))
            upd = lax.dot_general(kd, v, (((0,), (0,)), ((), ())), preferred_element_type=F32)
            sn_ref[g, hd] = _column_scale(jnp.exp(b_last), s_ref[g, hd]) + upd


def _gla_sample(p_s, la_s, state, gn, *, seq_len):
    rows = SUBLANES
    n, h, dk, dv = state.shape
    t = p_s.shape[0]
    per = rows // seq_len
    return pl.pallas_call(
        functools.partial(_gla_sample_kernel, seq_len=seq_len),
        grid=(t // rows,),
        in_specs=[
            pl.BlockSpec((rows, p_s.shape[1]), lambda i: (i, 0)),
            pl.BlockSpec((rows, la_s.shape[1]), lambda i: (i, 0)),
            pl.BlockSpec((per, h, dk, dv), lambda i: (i, 0, 0, 0)),
            pl.BlockSpec((1, dv), lambda i: (0, 0)),
        ],
        out_specs=[
            pl.BlockSpec((rows, h * dv), lambda i: (i, 0)),
            pl.BlockSpec((per, h, dk, dv), lambda i: (i, 0, 0, 0)),
        ],
        out_shape=[jax.ShapeDtypeStruct((t, h * dv), F32), jax.ShapeDtypeStruct(state.shape, F32)],
        compiler_params=_params("parallel"),
        name="gla_sample",
    )(p_s, la_s, state, gn.reshape(1, dv))


def _out_proj_kernel(a_ref, x_ref, w_ref, o_ref):
    o_ref[...] = x_ref[...] + _dot(a_ref[...], w_ref[...])


def _out_proj(a, x, w):
    t, d = x.shape
    kdim = a.shape[1]
    tm = _proj_tile(t)
    return pl.pallas_call(
        _out_proj_kernel,
        grid=(t // tm,),
        in_specs=[
            pl.BlockSpec((tm, kdim), lambda i: (i, 0)),
            pl.BlockSpec((tm, d), lambda i: (i, 0)),
            _resident((kdim, d), lambda i: (0, 0)),
        ],
        out_specs=pl.BlockSpec((tm, d), lambda i: (i, 0)),
        out_shape=jax.ShapeDtypeStruct((t, d), F32),
        compiler_params=_params("parallel"),
        name="gla_out_proj",
    )(a, x, w)


def kernel(x_prompt, x_sample, state_conv, state_gla, meta_tokens, norm_ffn, w_ffn_gate, w_ffn_up, w_ffn_down,
           norm_mix, conv_w_pw1, conv_b_pw1, conv_w_dw, conv_b_dw, conv_ln_g, conv_ln_b, conv_w_pw2, conv_b_pw2,
           gla_w_q, gla_w_k, gla_w_v, gla_w_g1, gla_w_g2, gla_b_g, gla_w_r, gla_gn_g, gla_w_o, norm_final):
    bp, sp, d = x_prompt.shape
    bs, ls, _ = x_sample.shape
    lp = N_META_ROWS + sp
    tp, ts = bp * lp, bs * ls
    n_pad = -(tp + ts) % TOKEN_ROW_MULTIPLE
    assert SUBLANES % ls == 0 and ts % SUBLANES == 0

    def with_pad(a):
        return jnp.concatenate([a, jnp.zeros((n_pad, a.shape[1]), F32)], axis=0) if n_pad else a

    def ffn(x, i, j, final_norm=False, windows=None):
        return _ffn(x, norm_ffn, w_ffn_gate, w_ffn_up, w_ffn_down, norm_final,
                    layer=i, slot=j, final_norm=final_norm, windows=windows)

    tm_p = max(r for r in range(SUBLANES, MAX_TOKEN_TILE + 1, SUBLANES) if sp % r == 0)
    per_seq = sp // tm_p
    n_win = bp * per_seq

    def prompt_window(i):
        return (i // per_seq) * lp + N_META_ROWS + (i % per_seq) * tm_p

    def prompt_rows(a, start):
        return jnp.stack([a[b * lp + start:(b + 1) * lp] for b in range(bp)])

    def sample_rows(a):
        return a[tp:tp + ts].reshape(ls, bs, -1)

    x = ffn(x_prompt.reshape(bp * sp, d), 0, 0, windows=(tm_p, n_win, None, prompt_window, tp + ts + n_pad))
    rest = ffn(jnp.concatenate([meta_tokens, jnp.transpose(x_sample, (1, 0, 2)).reshape(ts, d)], axis=0), 0, 0)
    for b in range(bp):
        x = lax.dynamic_update_slice(x, rest[:N_META_ROWS], (b * lp, 0))
    x = lax.dynamic_update_slice(x, with_pad(rest[N_META_ROWS:]), (tp, 0))
    u = _pw1_glu(x, norm_mix[0], conv_w_pw1[0], conv_b_pw1[0])
    u_s = sample_rows(u)
    y = _dwconv_prompt(u, conv_w_dw[0], conv_b_dw[0], n_seq=bp, seq_len=lp)
    y = _dwconv_sample(y, jnp.transpose(state_conv[0], (1, 0, 2)), u_s, conv_w_dw[0], conv_b_dw[0], row0=tp)
    x = _ln_pw2(y, x, conv_ln_g[0], conv_ln_b[0], conv_w_pw2[0], conv_b_pw2[0])
    x = ffn(x, 0, 1)
    new_conv_prompt = prompt_rows(u, lp - CONV_HALO)[None]
    new_conv_sample = jnp.concatenate([state_conv[0], jnp.transpose(u_s, (1, 0, 2))], axis=1)[None, :, ls:]

    x = ffn(x, 1, 0)
    p_qk, la = _gla_qk_decay(x, norm_mix[1], gla_w_q[0], gla_w_k[0], gla_w_g1[0], gla_w_g2[0], gla_b_g[0])
    p_v = _norm_proj(x, norm_mix[1], [gla_w_v[0]], name="gla_proj_v")
    p_r = _norm_proj(x, norm_mix[1], [gla_w_r[0]], name="gla_proj_r")
    og, new_gla_prompt = _gla_prompt(p_qk, p_v, p_r, la, gla_gn_g[0], n_seq=bp, seq_len=lp)

    def seq_major(a):
        return jnp.transpose(sample_rows(a), (1, 0, 2)).reshape(ts, -1)

    p_s = jnp.concatenate([seq_major(p_qk), seq_major(p_v), seq_major(p_r)], axis=1)
    og_s, new_gla_sample = _gla_sample(p_s, seq_major(la), state_gla[0], gla_gn_g[0], seq_len=ls)
    og_s = jnp.transpose(og_s.reshape(bs, ls, -1), (1, 0, 2)).reshape(ts, -1)
    og = lax.dynamic_update_slice(og, with_pad(og_s), (tp, 0))
    x = _out_proj(og, x, gla_w_o[0])

    y_prompt = ffn(x, 1, 1, final_norm=True, windows=(tm_p, n_win, prompt_window, None, n_win * tm_p))
    y_sample_t = ffn(x, 1, 1, final_norm=True, windows=(ts, 1, lambda i: tp + 0 * i, None, ts))
    y_prompt = y_prompt.reshape(bp, sp, d)
    y_sample = jnp.transpose(y_sample_t.reshape(ls, bs, d), (1, 0, 2))
    return (y_prompt, y_sample, new_conv_prompt, new_gla_prompt[None], new_conv_sample, new_gla_sample[None])
```
